```python
import jax, jax.numpy as jnp
from jax import lax
import numpy as np

D_MODEL = 1024
BATCH = 4
SEQ = 8192
DEPTH = 1

CHUNK = 64
N_LEFT_CHUNKS = 8
BAND = (N_LEFT_CHUNKS + 1) * CHUNK
REL_CLIP = 256
D_MIX = D_MODEL
HEAD_DIM = 64
D_RWKV = D_MIX // 2
D_ATTN = D_MIX - D_RWKV
H_RWKV = D_RWKV // HEAD_DIM
H_ATTN = D_ATTN // HEAD_DIM
DECAY_LORA = 64
AAA_LORA = 64
GATE_LORA = 128
D_FF = ((8 * D_MODEL // 3 + 127) // 128) * 128
N_RWKV_COLS = 3 * D_RWKV + DECAY_LORA + AAA_LORA + GATE_LORA
N_IN_COLS = N_RWKV_COLS + 3 * D_ATTN
N_MOD = 9
NORM_EPS = 1e-6
QK_EPS = 1e-6
LNX_EPS = 64e-5

kernel_name = "hybrid_rwkv7_chunkattn_macaron_adaln"


def rms_norm(x, g, eps):
    xf = x.astype(jnp.float32)
    y = xf * lax.rsqrt(jnp.mean(xf * xf, axis=-1, keepdims=True) + eps)
    return (y * g.astype(jnp.float32)).astype(x.dtype)


def modulate(n, shift, scale):
    return n * (1 + scale[:, None, :]) + shift[:, None, :]


def swiglu(n, w1, w3, w2):
    return (jax.nn.silu(n @ w1) * (n @ w3)) @ w2


def rwkv7_scan(r, w, k, v, kk, b):
    def step(state, inp):
        r_t, w_t, k_t, v_t, kk_t, b_t = inp
        sa = jnp.einsum('bhvk,bhk->bhv', state, -kk_t)
        state = (state * w_t[:, :, None, :] + sa[..., None] * b_t[:, :, None, :]
                 + v_t[..., None] * k_t[:, :, None, :])
        y_t = jnp.einsum('bhvk,bhk->bhv', state, r_t)
        return state, y_t

    bsz, _, nh, hd = r.shape
    seq_first = tuple(jnp.moveaxis(a, 1, 0) for a in (r, w, k, v, kk, b))
    state0 = jnp.zeros((bsz, nh, hd, hd), jnp.float32)
    _, ys = lax.scan(step, state0, seq_first)
    return jnp.moveaxis(ys, 0, 1)


def rwkv7_mixer(p, mu, w0, w_decay_up, a0, w_a_up, w_g_up, k_k, k_a, r_k, lnx_g, lnx_b):
    bsz, seq, _ = p.shape
    p = p.astype(jnp.float32)
    p_prev = jnp.pad(p[:, :-1], ((0, 0), (1, 0), (0, 0)))
    p = p + (p_prev - p) * mu.astype(jnp.float32)
    o = 3 * D_RWKV
    r, k, v = p[..., :D_RWKV], p[..., D_RWKV:2 * D_RWKV], p[..., 2 * D_RWKV:o]
    wd = p[..., o:o + DECAY_LORA]
    ad = p[..., o + DECAY_LORA:o + DECAY_LORA + AAA_LORA]
    gd = p[..., o + DECAY_LORA + AAA_LORA:]
    f32 = lambda t: t.astype(jnp.float32)
    w_pre = -jax.nn.softplus(-(f32(w0) + jnp.tanh(wd) @ f32(w_decay_up))) - 0.5
    decay = jnp.exp(-jnp.exp(w_pre))
    a = jax.nn.sigmoid(f32(a0) + ad @ f32(w_a_up))
    g = jax.nn.sigmoid(gd) @ f32(w_g_up)
    heads = lambda t: t.reshape(bsz, seq, H_RWKV, HEAD_DIM)
    kk = heads(k * f32(k_k))
    kk = kk / jnp.maximum(jnp.linalg.norm(kk, axis=-1, keepdims=True), 1e-12)
    k = k * (1 + (a - 1) * f32(k_a))
    r_h, k_h, v_h, a_h, w_h = heads(r), heads(k), heads(v), heads(a), heads(decay)
    y = rwkv7_scan(r_h, w_h, k_h, v_h, kk, kk * a_h)
    mean = jnp.mean(y, axis=-1, keepdims=True)
    var = jnp.mean(jnp.square(y - mean), axis=-1, keepdims=True)
    gn_g = f32(lnx_g).reshape(H_RWKV, HEAD_DIM)
    gn_b = f32(lnx_b).reshape(H_RWKV, HEAD_DIM)
    y = (y - mean) * lax.rsqrt(var + LNX_EPS) * gn_g + gn_b
    bonus = jnp.sum(r_h * k_h * f32(r_k), axis=-1, keepdims=True) * v_h
    return ((y + bonus).reshape(bsz, seq, D_RWKV) * g)


def chunk_band_attention(q, k, v, q_norm_g, k_norm_g, rel_bias):
    bsz, seq, _ = q.shape
    nc = seq // CHUNK
    to_chunks = lambda t: t.reshape(bsz, nc, CHUNK, H_ATTN, HEAD_DIM).transpose(0, 3, 1, 2, 4)
    q = rms_norm(to_chunks(q), q_norm_g, QK_EPS)
    k = rms_norm(to_chunks(k), k_norm_g, QK_EPS)
    v = to_chunks(v)
    padw = ((0, 0), (0, 0), (N_LEFT_CHUNKS, 0), (0, 0), (0, 0))
    k_pad, v_pad = jnp.pad(k, padw), jnp.pad(v, padw)
    k_band = jnp.concatenate([k_pad[:, :, j:j + nc] for j in range(N_LEFT_CHUNKS + 1)], axis=3)
    v_band = jnp.concatenate([v_pad[:, :, j:j + nc] for j in range(N_LEFT_CHUNKS + 1)], axis=3)
    band_j = np.repeat(np.arange(N_LEFT_CHUNKS + 1), CHUNK)
    kj = np.tile(np.arange(CHUNK), N_LEFT_CHUNKS + 1)
    qi = np.arange(CHUNK)[:, None]
    dist = (N_LEFT_CHUNKS - band_j)[None, :] * CHUNK + qi - kj[None, :]
    rel_idx = np.clip(dist, -REL_CLIP, REL_CLIP) + REL_CLIP
    bias = rel_bias[:, rel_idx].astype(jnp.float32)
    valid = (np.arange(nc)[:, None] - N_LEFT_CHUNKS + band_j[None, :]) >= 0
    scores = jnp.einsum('bhcqd,bhckd->bhcqk', q, k_band).astype(jnp.float32) * (HEAD_DIM ** -0.5)
    scores = scores + bias[None, :, None]
    scores = jnp.where(valid[None, None, :, None, :], scores, jnp.finfo(jnp.float32).min)
    probs = jax.nn.softmax(scores, axis=-1).astype(v.dtype)
    out = jnp.einsum('bhcqk,bhckd->bhcqd', probs, v_band)
    return out.transpose(0, 2, 3, 1, 4).reshape(bsz, seq, D_ATTN)


def setup_inputs(seed: int = 0) -> dict:
    key = jax.random.key(seed)
    ks = iter(jax.random.split(key, 32))
    L = DEPTH
    nrm = lambda shape, s: jax.random.normal(next(ks), shape, jnp.float32) * s
    gain = lambda shape: 1.0 + nrm(shape, 0.02)
    return {
        "x": nrm((BATCH, SEQ, D_MODEL), 1.0),
        "c": nrm((BATCH, D_MODEL), 1.0),
        "w_ada": nrm((L, D_MODEL, N_MOD * D_MODEL), D_MODEL ** -0.5),
        "b_ada": nrm((L, N_MOD * D_MODEL), 0.02),
        "norm1_g": gain((L, D_MODEL)),
        "ffn1_w1": nrm((L, D_MODEL, D_FF), D_MODEL ** -0.5),
        "ffn1_w3": nrm((L, D_MODEL, D_FF), D_MODEL ** -0.5),
        "ffn1_w2": nrm((L, D_FF, D_MODEL), D_FF ** -0.5),
        "norm2_g": gain((L, D_MODEL)),
        "w_in": nrm((L, D_MODEL, N_IN_COLS), D_MODEL ** -0.5),
        "mu_shift": jax.random.uniform(next(ks), (L, N_RWKV_COLS), jnp.float32),
        "w0": -2.0 + nrm((L, D_RWKV), 1.0),
        "w_decay_up": nrm((L, DECAY_LORA, D_RWKV), 0.5 * DECAY_LORA ** -0.5),
        "a0": nrm((L, D_RWKV), 0.1),
        "w_a_up": nrm((L, AAA_LORA, D_RWKV), AAA_LORA ** -0.5),
        "w_g_up": nrm((L, GATE_LORA, D_RWKV), GATE_LORA ** -0.5),
        "k_k": 0.85 + nrm((L, D_RWKV), 0.05),
        "k_a": 1.0 + nrm((L, D_RWKV), 0.05),
        "r_k": nrm((L, H_RWKV, HEAD_DIM), 0.1),
        "lnx_g": gain((L, D_RWKV)),
        "lnx_b": nrm((L, D_RWKV), 0.01),
        "q_norm_g": gain((L, HEAD_DIM)),
        "k_norm_g": gain((L, HEAD_DIM)),
        "rel_bias": nrm((L, H_ATTN, 2 * REL_CLIP + 1), 0.1),
        "w_out": nrm((L, D_MIX, D_MODEL), D_MIX ** -0.5),
        "norm3_g": gain((L, D_MODEL)),
        "ffn2_w1": nrm((L, D_MODEL, D_FF), D_MODEL ** -0.5),
        "ffn2_w3": nrm((L, D_MODEL, D_FF), D_MODEL ** -0.5),
        "ffn2_w2": nrm((L, D_FF, D_MODEL), D_FF ** -0.5),
    }


def reference(x, c, w_ada, b_ada, norm1_g, ffn1_w1, ffn1_w3, ffn1_w2, norm2_g, w_in,
              mu_shift, w0, w_decay_up, a0, w_a_up, w_g_up, k_k, k_a, r_k, lnx_g, lnx_b,
              q_norm_g, k_norm_g, rel_bias, w_out, norm3_g, ffn2_w1, ffn2_w3, ffn2_w2):
    bsz = x.shape[0]
    h = x
    for l in range(DEPTH):
        mod = (jax.nn.silu(c) @ w_ada[l] + b_ada[l]).reshape(bsz, N_MOD, D_MODEL)
        sh1, sc1, g1, sh2, sc2, g2, sh3, sc3, g3 = [mod[:, i] for i in range(N_MOD)]
        n1 = modulate(rms_norm(h, norm1_g[l], NORM_EPS), sh1, sc1)
        h = h + 0.5 * g1[:, None, :] * swiglu(n1, ffn1_w1[l], ffn1_w3[l], ffn1_w2[l])
        n2 = modulate(rms_norm(h, norm2_g[l], NORM_EPS), sh2, sc2)
        proj = n2 @ w_in[l]
        y_rwkv = rwkv7_mixer(proj[..., :N_RWKV_COLS], mu_shift[l], w0[l], w_decay_up[l],
                             a0[l], w_a_up[l], w_g_up[l], k_k[l], k_a[l], r_k[l],
                             lnx_g[l], lnx_b[l]).astype(h.dtype)
        o = N_RWKV_COLS
        y_attn = chunk_band_attention(proj[..., o:o + D_ATTN], proj[..., o + D_ATTN:o + 2 * D_ATTN],
                                      proj[..., o + 2 * D_ATTN:], q_norm_g[l], k_norm_g[l],
                                      rel_bias[l]).astype(h.dtype)
        mixed = jnp.concatenate([y_rwkv, y_attn], axis=-1) @ w_out[l]
        h = h + g2[:, None, :] * mixed
        n3 = modulate(rms_norm(h, norm3_g[l], NORM_EPS), sh3, sc3)
        h = h + 0.5 * g3[:, None, :] * swiglu(n3, ffn2_w1[l], ffn2_w3[l], ffn2_w2[l])
    return h
```

```python
import functools

import numpy as np
import jax
import jax.numpy as jnp
from jax import lax
from jax.experimental import pallas as pl
from jax.experimental.pallas import tpu as pltpu

F32 = jnp.float32
BF16 = jnp.bfloat16

HEAD_DIM = 64
CHUNK = 64
N_LEFT_CHUNKS = 8
BAND = (N_LEFT_CHUNKS + 1) * CHUNK
REL_CLIP = 256
N_MOD = 9
NORM_EPS = 1e-6
QK_EPS = 1e-6
LNX_EPS = 64e-5
LANES = 128
VMEM_LIMIT = 56 * 1024 * 1024

_NN = (((1,), (0,)), ((), ()))
_NT = (((1,), (1,)), ((), ()))
_TN = (((0,), (0,)), ((), ()))


def _split(x, n):
    if x.dtype == BF16:
        return [x]
    parts = []
    r = x
    for i in range(n):
        h = r.astype(BF16)
        parts.append(h)
        if i + 1 < n:
            r = r - h.astype(F32)
    return parts


def _mm(a, b, dims=_NN, pa=1, pb=1):
    ap, bp = _split(a, pa), _split(b, pb)
    order = max(len(ap), len(bp))
    acc = None
    for i, ai in enumerate(ap):
        for j, bj in enumerate(bp):
            if i + j < order:
                t = lax.dot_general(ai, bj, dims, preferred_element_type=F32)
                acc = t if acc is None else acc + t
    return acc


def _sigmoid(x):
    return 1.0 / (1.0 + jnp.exp(-x))


def _iota(shape, axis):
    return lax.broadcasted_iota(jnp.int32, shape, axis)


def _mod_body(c_ref, w_ref, b_ref, o_ref):
    c = c_ref[...]
    s = c * _sigmoid(c)
    o_ref[...] = _mm(s, w_ref[...], pa=2, pb=2) + b_ref[...]


def _mod_call(c, w_ada, b_ada):
    bsz, d = c.shape
    n = w_ada.shape[1]
    tn = 1024
    return pl.pallas_call(
        _mod_body,
        out_shape=jax.ShapeDtypeStruct((bsz, n), F32),
        grid=(n // tn,),
        in_specs=[pl.BlockSpec((bsz, d), lambda j: (0, 0)),
                  pl.BlockSpec((d, tn), lambda j: (0, j)),
                  pl.BlockSpec((1, tn), lambda j: (0, j))],
        out_specs=pl.BlockSpec((bsz, tn), lambda j: (0, j)),
        compiler_params=pltpu.CompilerParams(dimension_semantics=("arbitrary",),
                                             vmem_limit_bytes=VMEM_LIMIT),
        name="adaln_mod",
    )(c, w_ada, b_ada.reshape(1, n))


def _norm_mod(h, ng, sh, sc):
    y = h * lax.rsqrt(jnp.mean(h * h, axis=-1, keepdims=True) + NORM_EPS)
    return (y * ng) * (1.0 + sc) + sh


def _ffn_body(premix, ff_tile, *refs):
    if premix:
        h_ref, yr_ref, ya_ref, mod_ref, ng_ref, wo_ref, w1_ref, w3_ref, w2_ref, o_ref, act_ref = refs
    else:
        h_ref, mod_ref, ng_ref, w1_ref, w3_ref, w2_ref, o_ref, act_ref = refs
    h = h_ref[0]
    mod = mod_ref[0]
    if premix:
        dr = yr_ref.shape[-1]
        mixed = (_mm(yr_ref[0], wo_ref[0:dr, :]) + _mm(ya_ref[0], wo_ref[dr:, :]))
        h = h + mod[5:6, :] * mixed
        sh, sc, g = mod[6:7, :], mod[7:8, :], mod[8:9, :]
    else:
        sh, sc, g = mod[0:1, :], mod[1:2, :], mod[2:3, :]
    n = _norm_mod(h, ng_ref[...], sh, sc).astype(BF16)
    d_ff = w1_ref.shape[1]
    for j in range(d_ff // ff_tile):
        js = slice(j * ff_tile, (j + 1) * ff_tile)
        a = _mm(n, w1_ref[:, js])
        b = _mm(n, w3_ref[:, js])
        act_ref[:, js] = (a * _sigmoid(a) * b).astype(BF16)
    o_ref[0] = h + (0.5 * g) * _mm(act_ref[...], w2_ref[...])


def _resident(shape):
    return pl.BlockSpec(shape, lambda b, i: (0,) * len(shape), pipeline_mode=pl.Buffered(1))


def _ffn_call(h, mod, norm_g, w1, w3, w2, premix=None, tm=512, ff_tile=256):
    bsz, seq, d = h.shape
    d_ff = w1.shape[1]
    tile = lambda width: pl.BlockSpec((1, tm, width), lambda b, i: (b, i, 0))
    mod_spec = pl.BlockSpec((1, N_MOD, d), lambda b, i: (b, 0, 0))
    in_specs = [tile(d)]
    args = [h]
    if premix is not None:
        y_r, y_a, w_out = premix
        in_specs += [tile(y_r.shape[-1]), tile(y_a.shape[-1])]
        args += [y_r, y_a]
    in_specs += [mod_spec, _resident((1, d))]
    args += [mod, norm_g.reshape(1, d)]
    if premix is not None:
        in_specs.append(_resident(w_out.shape))
        args.append(w_out)
    in_specs += [_resident(w1.shape), _resident(w3.shape), _resident(w2.shape)]
    args += [w1, w3, w2]
    return pl.pallas_call(
        functools.partial(_ffn_body, premix is not None, ff_tile),
        out_shape=jax.ShapeDtypeStruct((bsz, seq, d), F32),
        grid=(bsz, seq // tm),
        in_specs=in_specs,
        out_specs=tile(d),
        scratch_shapes=[pltpu.VMEM((tm, d_ff), BF16)],
        compiler_params=pltpu.CompilerParams(dimension_semantics=("parallel", "parallel"),
                                             vmem_limit_bytes=VMEM_LIMIT),
        name="ffn_premix" if premix is not None else "ffn",
    )(*args)


def _proj_body(h_ref, mod_ref, ng_ref, wr_ref, wa_ref, p_ref, qkv_ref):
    mod = mod_ref[0]
    n = _norm_mod(h_ref[0], ng_ref[...], mod[3:4, :], mod[4:5, :]).astype(BF16)
    p_ref[0] = _mm(n, wr_ref[...])
    qkv_ref[0] = _mm(n, wa_ref[...]).astype(BF16)


def _proj_call(h, mod, norm_g, w_r, w_a, tm=512):
    bsz, seq, d = h.shape
    tile = lambda width: pl.BlockSpec((1, tm, width), lambda b, i: (b, i, 0))
    return pl.pallas_call(
        _proj_body,
        out_shape=(jax.ShapeDtypeStruct((bsz, seq, w_r.shape[1]), F32),
                   jax.ShapeDtypeStruct((bsz, seq, w_a.shape[1]), BF16)),
        grid=(bsz, seq // tm),
        in_specs=[tile(d), pl.BlockSpec((1, N_MOD, d), lambda b, i: (b, 0, 0)), _resident((1, d)),
                  _resident(w_r.shape), _resident(w_a.shape)],
        out_specs=(tile(w_r.shape[1]), tile(w_a.shape[1])),
        compiler_params=pltpu.CompilerParams(dimension_semantics=("parallel", "parallel"),
                                             vmem_limit_bytes=VMEM_LIMIT),
        name="in_proj",
    )(h, mod, norm_g.reshape(1, d), w_r, w_a)


def _solve_unit_lower(n_mat, x):
    size = n_mat.shape[0]
    r, c = _iota((size, size), 0), _iota((size, size), 1)
    same_block = (r // 16) == (c // 16)
    nd = jnp.where(same_block, n_mat, 0.0)
    no = n_mat - nd
    y = jnp.concatenate([x, no], axis=1)
    pw = nd
    for step in range(4):
        y = y + _mm(pw, y, pa=2, pb=2)
        if step < 3:
            pw = _mm(pw, pw, pa=2, pb=2)
    tx, p = y[:, :x.shape[1]], y[:, x.shape[1]:]
    p2 = _mm(p, p, pa=2, pb=2)
    tx = tx + _mm(p2, tx, pa=2, pb=2)
    return tx + _mm(p, tx, pa=2, pb=2)


def _rwkv_pair(at, rt, bc, kc, bh, kh, v, g_last, z):
    size = at.shape[0]
    lane = _iota((size, LANES), 1)
    row = _iota((size, LANES), 0)
    col = lane % HEAD_DIM
    head0 = lane < HEAD_DIM
    head0_wide = jnp.concatenate([head0, head0], axis=1)
    lane2 = _iota((LANES, LANES), 1)
    row2 = _iota((LANES, LANES), 0)
    zeros = jnp.zeros((size, LANES), F32)

    ar = jnp.concatenate([at, rt], axis=0)
    bk = jnp.concatenate([bc, kc], axis=0)
    v0 = jnp.concatenate([zeros, v], axis=0)
    tops, bots, mvs = [], [], []
    for h in range(2):
        in_head = (lane2 < HEAD_DIM) if h == 0 else (lane2 >= HEAD_DIM)
        m = _mm(jnp.where(in_head, ar, 0.0), bk, _NT)
        top = jnp.where(col < row, m[:size], 0.0)
        tops.append(top)
        bots.append(jnp.where(col <= row, m[size:], 0.0))
        mvs.append(_mm(top, v0))
    x = jnp.concatenate([at, jnp.where(head0, mvs[0], mvs[1])], axis=1)
    tx = [_solve_unit_lower(tops[h][:, :HEAD_DIM], x) for h in range(2)]
    wu = jnp.where(head0_wide, tx[0], tx[1])
    w, u0 = wu[:, :LANES], wu[:, LANES:]
    rhs = jnp.concatenate([wu, jnp.concatenate([zeros, v], axis=1)], axis=0)
    qy = jnp.where(head0_wide, _mm(bots[0], rhs), _mm(bots[1], rhs))
    q = rt + qy[:, :LANES]
    y0 = qy[:, LANES:]
    block_diag = (lane2 < HEAD_DIM) == (row2 < HEAD_DIM)
    g_mat = (jnp.where(block_diag, _mm(bh, w, _TN), 0.0)
             + jnp.where(lane2 == row2, g_last, 0.0))
    d_mat = jnp.where(block_diag,
                      _mm(jnp.concatenate([bh, kh], axis=0), jnp.concatenate([u0, v], axis=0), _TN),
                      0.0)
    y = _mm(q, z, pa=2, pb=2) + y0
    z_new = _mm(g_mat, z, pa=2, pb=2) + d_mat
    return y, z_new


def _rwkv_chunk(p, prev_row, zs, prm, roll):
    mu, vecs, wd_pad, wa_pad, wg, bd, ltri = prm
    size = p.shape[0]
    d_r = vecs.shape[1]
    w0, a0, k_k, k_a, r_k, lnx_g, lnx_b = [vecs[i:i + 1, :] for i in range(7)]
    first = _iota(p.shape, 0) == 0
    p_prev = jnp.where(first, prev_row, roll(p, 1, 0))
    ps = p + (p_prev - p) * mu
    r, k, v = ps[:, 0:d_r], ps[:, d_r:2 * d_r], ps[:, 2 * d_r:3 * d_r]
    lora = ps[:, 3 * d_r:3 * d_r + LANES]
    gd = ps[:, 3 * d_r + LANES:]
    z_dec = -(w0 + _mm(jnp.tanh(lora), wd_pad, pa=2, pb=2))
    softplus = jnp.maximum(z_dec, 0.0) + jnp.log(1.0 + jnp.exp(-jnp.abs(z_dec)))
    lw = -jnp.exp(-softplus - 0.5)
    a = _sigmoid(a0 + _mm(lora, wa_pad, pa=2, pb=2))
    g = _mm(_sigmoid(gd), wg, pa=2, pb=2)
    kk = k * k_k
    kk = kk / jnp.maximum(jnp.sqrt(_mm(kk * kk, bd, pa=2)), 1e-12)
    k2 = k * (1.0 + (a - 1.0) * k_a)
    bv = kk * a
    bonus = _mm(r * k2 * r_k, bd, pa=2) * v
    c = _mm(ltri, lw, pb=3)
    c_last = c[size - 1:size, :]
    e_neg = jnp.exp(-c)
    e_tail = jnp.exp(c_last - c)
    at = -kk * jnp.exp(c - lw)
    rt = r * jnp.exp(c)
    bc, kc = bv * e_neg, k2 * e_neg
    bh, kh = bv * e_tail, k2 * e_tail
    g_last = jnp.exp(c_last)
    ys, zs_new = [], []
    for pr in range(d_r // LANES):
        sl = slice(pr * LANES, (pr + 1) * LANES)
        y, z_new = _rwkv_pair(at[:, sl], rt[:, sl], bc[:, sl], kc[:, sl], bh[:, sl], kh[:, sl],
                              v[:, sl], g_last[:, sl], zs[pr])
        ys.append(y)
        zs_new.append(z_new)
    y = jnp.concatenate(ys, axis=1)
    inv = 1.0 / HEAD_DIM
    mean = _mm(y, bd, pa=2) * inv
    dlt = y - mean
    var = _mm(dlt * dlt, bd, pa=2) * inv
    y = dlt * lax.rsqrt(var + LNX_EPS) * lnx_g + lnx_b
    return (y + bonus) * g, zs_new


def _rwkv_body(p_ref, mu_ref, vec_ref, wd_ref, wa_ref, wg_ref, bd_ref, lt_ref, o_ref, z_ref, carry_ref):
    @pl.when(pl.program_id(1) == 0)
    def _():
        z_ref[...] = jnp.zeros_like(z_ref)
        carry_ref[...] = jnp.zeros_like(carry_ref)

    prm = (mu_ref[...], vec_ref[...], wd_ref[...], wa_ref[...], wg_ref[...], bd_ref[...], lt_ref[...])
    n_pairs = z_ref.shape[0]
    zs = [z_ref[i] for i in range(n_pairs)]
    prev_row = carry_ref[...]
    for ci in range(p_ref.shape[1] // CHUNK):
        p = p_ref[0, ci * CHUNK:(ci + 1) * CHUNK, :]
        out, zs = _rwkv_chunk(p, prev_row, zs, prm, pltpu.roll)
        o_ref[0, ci * CHUNK:(ci + 1) * CHUNK, :] = out.astype(o_ref.dtype)
        prev_row = p[CHUNK - 1:CHUNK, :]
    for i in range(n_pairs):
        z_ref[i] = zs[i]
    carry_ref[...] = prev_row


def _rwkv_call(p, mu, vecs, wd_pad, wa_pad, wg, chunks_per_step=1):
    bsz, seq, n_cols = p.shape
    d_r = vecs.shape[1]
    ts = CHUNK * chunks_per_step
    head = np.arange(d_r) // HEAD_DIM
    bd = jnp.asarray(head[:, None] == head[None, :], BF16)
    ltri = jnp.asarray(np.tril(np.ones((CHUNK, CHUNK))), BF16)
    full = lambda a: pl.BlockSpec(a.shape, lambda b, i: (0,) * a.ndim)
    consts = (mu, vecs, wd_pad, wa_pad, wg, bd, ltri)
    return pl.pallas_call(
        _rwkv_body,
        out_shape=jax.ShapeDtypeStruct((bsz, seq, d_r), BF16),
        grid=(bsz, seq // ts),
        in_specs=[pl.BlockSpec((1, ts, n_cols), lambda b, i: (b, i, 0))] + [full(a) for a in consts],
        out_specs=pl.BlockSpec((1, ts, d_r), lambda b, i: (b, i, 0)),
        scratch_shapes=[pltpu.VMEM((d_r // LANES, LANES, LANES), F32), pltpu.VMEM((1, n_cols), F32)],
        compiler_params=pltpu.CompilerParams(dimension_semantics=("arbitrary", "arbitrary"),
                                             vmem_limit_bytes=VMEM_LIMIT),
        name="rwkv7_chunked",
    )(p, *consts)


def _head_rms(x, gain, bd, eps):
    xf = x.astype(F32)
    ms = _mm(xf * xf, bd) * (1.0 / HEAD_DIM)
    return xf * lax.rsqrt(ms + eps) * gain


def _attn_body(q_ref, kp_ref, kc_ref, vp_ref, vc_ref, bias_ref, qg_ref, kg_ref, bd_ref, o_ref,
               qn_ref, kn_ref, vn_ref):
    j = pl.program_id(1)
    tq = q_ref.shape[1]
    d_a = q_ref.shape[2]
    bd = bd_ref[...]
    qn_ref[...] = (_head_rms(q_ref[0], qg_ref[...], bd, QK_EPS) * (HEAD_DIM ** -0.5)).astype(BF16)
    kn_ref[0:tq, :] = _head_rms(kp_ref[0], kg_ref[...], bd, QK_EPS).astype(BF16)
    kn_ref[tq:, :] = _head_rms(kc_ref[0], kg_ref[...], bd, QK_EPS).astype(BF16)
    vn_ref[0:tq, :] = vp_ref[0]
    vn_ref[tq:, :] = vc_ref[0]
    lane = _iota((CHUNK, LANES), 1)
    head0 = lane < HEAD_DIM
    key = _iota((CHUNK, BAND), 1)
    first_tile = j == 0
    left = tq - N_LEFT_CHUNKS * CHUNK

    def chunk_step(ci, carry):
        q0 = pl.multiple_of(ci * CHUNK, CHUNK)
        k0 = pl.multiple_of(ci * CHUNK + left, CHUNK)
        invalid = jnp.logical_and(first_tile, key < (N_LEFT_CHUNKS - ci) * CHUNK)
        for pr in range(d_a // LANES):
            sl = slice(pr * LANES, (pr + 1) * LANES)
            qp = qn_ref[pl.ds(q0, CHUNK), sl]
            kb = kn_ref[pl.ds(k0, BAND), sl]
            vb = vn_ref[pl.ds(k0, BAND), sl]
            outs = []
            for h in range(2):
                qh = jnp.where(head0 if h == 0 else jnp.logical_not(head0), qp, jnp.zeros_like(qp))
                s = _mm(qh, kb, _NT) + bias_ref[2 * pr + h]
                s = jnp.where(invalid, -1e30, s)
                e = jnp.exp(s - jnp.max(s, axis=-1, keepdims=True))
                o = _mm(e.astype(BF16), vb)
                outs.append(o / jnp.sum(e, axis=-1, keepdims=True))
            o_ref[0, pl.ds(q0, CHUNK), sl] = jnp.where(head0, outs[0], outs[1]).astype(o_ref.dtype)
        return carry

    lax.fori_loop(0, tq // CHUNK, chunk_step, 0)


def _attn_call(qkv, bias, q_norm_g, k_norm_g, tq=512):
    bsz, seq, three_d = qkv.shape
    d_a = three_d // 3
    n_heads = d_a // HEAD_DIM
    assert tq >= N_LEFT_CHUNKS * CHUNK and seq % tq == 0
    head = np.arange(d_a) // HEAD_DIM
    bd = jnp.asarray(head[:, None] == head[None, :], BF16)
    cur = lambda col: pl.BlockSpec((1, tq, d_a), lambda b, i: (b, i, col))
    prev = lambda col: pl.BlockSpec((1, tq, d_a), lambda b, i: (b, jnp.maximum(i - 1, 0), col))
    full = lambda a: pl.BlockSpec(a.shape, lambda b, i: (0,) * a.ndim)
    qg = jnp.tile(q_norm_g.astype(F32), n_heads).reshape(1, d_a)
    kg = jnp.tile(k_norm_g.astype(F32), n_heads).reshape(1, d_a)
    return pl.pallas_call(
        _attn_body,
        out_shape=jax.ShapeDtypeStruct((bsz, seq, d_a), BF16),
        grid=(bsz, seq // tq),
        in_specs=[cur(0), prev(1), cur(1), prev(2), cur(2), full(bias), full(qg), full(kg), full(bd)],
        out_specs=pl.BlockSpec((1, tq, d_a), lambda b, i: (b, i, 0)),
        scratch_shapes=[pltpu.VMEM((tq, d_a), BF16), pltpu.VMEM((2 * tq, d_a), BF16),
                        pltpu.VMEM((2 * tq, d_a), BF16)],
        compiler_params=pltpu.CompilerParams(dimension_semantics=("parallel", "parallel"),
                                             vmem_limit_bytes=VMEM_LIMIT),
        name="chunk_band_attn",
    )(qkv, qkv, qkv, qkv, qkv, bias, qg, kg, bd)


def _rel_bias_table(rel_bias):
    band_j = np.repeat(np.arange(N_LEFT_CHUNKS + 1), CHUNK)
    kj = np.tile(np.arange(CHUNK), N_LEFT_CHUNKS + 1)
    qi = np.arange(CHUNK)[:, None]
    dist = (N_LEFT_CHUNKS - band_j)[None, :] * CHUNK + qi - kj[None, :]
    rel_idx = np.clip(dist, -REL_CLIP, REL_CLIP) + REL_CLIP
    return rel_bias[:, rel_idx].astype(F32)


def kernel(x, c, w_ada, b_ada, norm1_g, ffn1_w1, ffn1_w3, ffn1_w2, norm2_g, w_in, mu_shift, w0, w_decay_up, a0, w_a_up, w_g_up, k_k, k_a, r_k, lnx_g, lnx_b, q_norm_g, k_norm_g, rel_bias, w_out, norm3_g, ffn2_w1, ffn2_w3, ffn2_w2):
    bsz, seq, d = x.shape
    depth = w_ada.shape[0]
    d_r = w0.shape[1]
    n_lora_w, n_lora_a = w_decay_up.shape[1], w_a_up.shape[1]
    assert n_lora_w + n_lora_a == LANES and w_g_up.shape[1] == LANES
    n_rwkv = 3 * d_r + n_lora_w + n_lora_a + w_g_up.shape[1]
    bf = lambda w: w.astype(BF16)
    h = x
    for l in range(depth):
        mod = _mod_call(c, w_ada[l], b_ada[l]).reshape(bsz, N_MOD, d)
        h = _ffn_call(h, mod, norm1_g[l], bf(ffn1_w1[l]), bf(ffn1_w3[l]), bf(ffn1_w2[l]))
        p, qkv = _proj_call(h, mod, norm2_g[l], bf(w_in[l][:, :n_rwkv]), bf(w_in[l][:, n_rwkv:]))
        zeros_w = jnp.zeros((n_lora_a, d_r), F32)
        zeros_a = jnp.zeros((n_lora_w, d_r), F32)
        vecs = jnp.stack([w0[l], a0[l], k_k[l], k_a[l], r_k[l].reshape(-1), lnx_g[l], lnx_b[l],
                          jnp.zeros_like(w0[l])]).astype(F32)
        y_r = _rwkv_call(p, mu_shift[l].reshape(1, -1), vecs,
                         jnp.concatenate([w_decay_up[l], zeros_w], axis=0),
                         jnp.concatenate([zeros_a, w_a_up[l]], axis=0), w_g_up[l])
        y_a = _attn_call(qkv, _rel_bias_table(rel_bias[l]), q_norm_g[l], k_norm_g[l])
        h = _ffn_call(h, mod, norm3_g[l], bf(ffn2_w1[l]), bf(ffn2_w3[l]), bf(ffn2_w2[l]),
                      premix=(y_r, y_a, bf(w_out[l])))
    return h
```

```python
import functools

import numpy as np
import jax
import jax.numpy as jnp
from jax import lax
from jax.experimental import pallas as pl
from jax.experimental.pallas import tpu as pltpu

F32 = jnp.float32
BF16 = jnp.bfloat16

HEAD_DIM = 64
CHUNK = 64
N_LEFT_CHUNKS = 8
BAND = (N_LEFT_CHUNKS + 1) * CHUNK
REL_CLIP = 256
N_MOD = 9
NORM_EPS = 1e-6
QK_EPS = 1e-6
LNX_EPS = 64e-5
LANES = 128
VMEM_LIMIT = 56 * 1024 * 1024

_NN = (((1,), (0,)), ((), ()))
_NT = (((1,), (1,)), ((), ()))
_TN = (((0,), (0,)), ((), ()))


def _split(x, n):
    if x.dtype == BF16:
        return [x]
    parts = []
    r = x
    for i in range(n):
        h = r.astype(BF16)
        parts.append(h)
        if i + 1 < n:
            r = r - h.astype(F32)
    return parts


def _mm(a, b, dims=_NN, pa=1, pb=1):
    ap, bp = _split(a, pa), _split(b, pb)
    order = max(len(ap), len(bp))
    acc = None
    for i, ai in enumerate(ap):
        for j, bj in enumerate(bp):
            if i + j < order:
                t = lax.dot_general(ai, bj, dims, preferred_element_type=F32)
                acc = t if acc is None else acc + t
    return acc


def _sigmoid(x):
    return 1.0 / (1.0 + jnp.exp(-x))


def _iota(shape, axis):
    return lax.broadcasted_iota(jnp.int32, shape, axis)


def _mod_body(c_ref, w_ref, b_ref, o_ref):
    c = c_ref[...]
    s = c * _sigmoid(c)
    o_ref[...] = _mm(s, w_ref[...], pa=2, pb=2) + b_ref[...]


def _mod_call(c, w_ada, b_ada):
    bsz, d = c.shape
    n = w_ada.shape[1]
    tn = 1024
    return pl.pallas_call(
        _mod_body,
        out_shape=jax.ShapeDtypeStruct((bsz, n), F32),
        grid=(n // tn,),
        in_specs=[pl.BlockSpec((bsz, d), lambda j: (0, 0)),
                  pl.BlockSpec((d, tn), lambda j: (0, j)),
                  pl.BlockSpec((1, tn), lambda j: (0, j))],
        out_specs=pl.BlockSpec((bsz, tn), lambda j: (0, j)),
        compiler_params=pltpu.CompilerParams(dimension_semantics=("arbitrary",),
                                             vmem_limit_bytes=VMEM_LIMIT),
        name="adaln_mod",
    )(c, w_ada, b_ada.reshape(1, n))


def _norm_mod(h, ng, sh, sc):
    y = h * lax.rsqrt(jnp.mean(h * h, axis=-1, keepdims=True) + NORM_EPS)
    return (y * ng) * (1.0 + sc) + sh


def _ffn_body(premix, ff_tile, *refs):
    if premix:
        h_ref, yr_ref, ya_ref, mod_ref, ng_ref, wo_ref, w1_ref, w3_ref, w2_ref, o_ref, act_ref = refs
    else:
        h_ref, mod_ref, ng_ref, w1_ref, w3_ref, w2_ref, o_ref, act_ref = refs
    h = h_ref[0]
    mod = mod_ref[0]
    if premix:
        dr = yr_ref.shape[-1]
        mixed = (_mm(yr_ref[0], wo_ref[0:dr, :]) + _mm(ya_ref[0], wo_ref[dr:, :]))
        h = h + mod[5:6, :] * mixed
        sh, sc, g = mod[6:7, :], mod[7:8, :], mod[8:9, :]
    else:
        sh, sc, g = mod[0:1, :], mod[1:2, :], mod[2:3, :]
    n = _norm_mod(h, ng_ref[...], sh, sc).astype(BF16)
    d_ff = w1_ref.shape[1]
    for j in range(d_ff // ff_tile):
        js = slice(j * ff_tile, (j + 1) * ff_tile)
        a = _mm(n, w1_ref[:, js])
        b = _mm(n, w3_ref[:, js])
        act_ref[:, js] = (a * _sigmoid(a) * b).astype(BF16)
    o_ref[0] = h + (0.5 * g) * _mm(act_ref[...], w2_ref[...])


def _resident(shape):
    return pl.BlockSpec(shape, lambda b, i: (0,) * len(shape), pipeline_mode=pl.Buffered(1))


def _ffn_call(h, mod, norm_g, w1, w3, w2, premix=None, tm=512, ff_tile=256):
    bsz, seq, d = h.shape
    d_ff = w1.shape[1]
    tile = lambda width: pl.BlockSpec((1, tm, width), lambda b, i: (b, i, 0))
    mod_spec = pl.BlockSpec((1, N_MOD, d), lambda b, i: (b, 0, 0))
    in_specs = [tile(d)]
    args = [h]
    if premix is not None:
        y_r, y_a, w_out = premix
        in_specs += [tile(y_r.shape[-1]), tile(y_a.shape[-1])]
        args += [y_r, y_a]
    in_specs += [mod_spec, _resident((1, d))]
    args += [mod, norm_g.reshape(1, d)]
    if premix is not None:
        in_specs.append(_resident(w_out.shape))
        args.append(w_out)
    in_specs += [_resident(w1.shape), _resident(w3.shape), _resident(w2.shape)]
    args += [w1, w3, w2]
    return pl.pallas_call(
        functools.partial(_ffn_body, premix is not None, ff_tile),
        out_shape=jax.ShapeDtypeStruct((bsz, seq, d), F32),
        grid=(bsz, seq // tm),
        in_specs=in_specs,
        out_specs=tile(d),
        scratch_shapes=[pltpu.VMEM((tm, d_ff), BF16)],
        compiler_params=pltpu.CompilerParams(dimension_semantics=("parallel", "parallel"),
                                             vmem_limit_bytes=VMEM_LIMIT),
        name="ffn_premix" if premix is not None else "ffn",
    )(*args)


def _proj_body(h_ref, mod_ref, ng_ref, wr_ref, wa_ref, p_ref, qkv_ref):
    mod = mod_ref[0]
    n = _norm_mod(h_ref[0], ng_ref[...], mod[3:4, :], mod[4:5, :]).astype(BF16)
    p_ref[0] = _mm(n, wr_ref[...])
    qkv_ref[0] = _mm(n, wa_ref[...]).astype(BF16)


def _proj_call(h, mod, norm_g, w_r, w_a, tm=512):
    bsz, seq, d = h.shape
    tile = lambda width: pl.BlockSpec((1, tm, width), lambda b, i: (b, i, 0))
    return pl.pallas_call(
        _proj_body,
        out_shape=(jax.ShapeDtypeStruct((bsz, seq, w_r.shape[1]), F32),
                   jax.ShapeDtypeStruct((bsz, seq, w_a.shape[1]), BF16)),
        grid=(bsz, seq // tm),
        in_specs=[tile(d), pl.BlockSpec((1, N_MOD, d), lambda b, i: (b, 0, 0)), _resident((1, d)),
                  _resident(w_r.shape), _resident(w_a.shape)],
        out_specs=(tile(w_r.shape[1]), tile(w_a.shape[1])),
        compiler_params=pltpu.CompilerParams(dimension_semantics=("parallel", "parallel"),
                                             vmem_limit_bytes=VMEM_LIMIT),
        name="in_proj",
    )(h, mod, norm_g.reshape(1, d), w_r, w_a)


SOLVE_TERMS = (1, 1)
STATE_TERMS = (2, 2)


def _solve_unit_lower(n_mats, xs):
    size = n_mats[0].shape[0]
    r, c = _iota((size, size), 0), _iota((size, size), 1)
    same_block = (r // 16) == (c // 16)
    eye = jnp.where(r == c, 1.0, 0.0).astype(F32)
    mm = functools.partial(_mm, pa=SOLVE_TERMS[0], pb=SOLVE_TERMS[1])
    nd = [jnp.where(same_block, n, 0.0) for n in n_mats]
    no = [n - d for n, d in zip(n_mats, nd)]
    inv = [eye + d for d in nd]
    pw = nd
    for _ in range(3):
        pw = [mm(w, w) for w in pw]
        inv = [t + mm(w, t) for w, t in zip(pw, inv)]
    p = [mm(t, o) for t, o in zip(inv, no)]
    tx = [mm(t, x) for t, x in zip(inv, xs)]
    p2 = [mm(q, q) for q in p]
    tx = [t + mm(q, t) for q, t in zip(p2, tx)]
    return [t + mm(q, t) for q, t in zip(p, tx)]


def _rwkv_local(items):
    size = items[0]["at"].shape[0]
    lane = _iota((size, LANES), 1)
    row = _iota((size, LANES), 0)
    col = lane % HEAD_DIM
    head0 = lane < HEAD_DIM
    head0_wide = jnp.concatenate([head0, head0], axis=1)
    lane2 = _iota((LANES, LANES), 1)
    row2 = _iota((LANES, LANES), 0)
    in_head = (lane2 < HEAD_DIM, lane2 >= HEAD_DIM)
    block_diag = (lane2 < HEAD_DIM) == (row2 < HEAD_DIM)
    zeros = jnp.zeros((size, LANES), F32)
    heads = [(i, h) for i in range(len(items)) for h in range(2)]

    ar = [jnp.concatenate([it["at"], it["rt"]], axis=0) for it in items]
    bk = [jnp.concatenate([it["bc"], it["kc"]], axis=0) for it in items]
    v0 = [jnp.concatenate([zeros, it["v"]], axis=0) for it in items]
    m = [_mm(jnp.where(in_head[h], ar[i], 0.0), bk[i], _NT) for i, h in heads]
    tops = [jnp.where(col < row, mh[:size], 0.0) for mh in m]
    bots = [jnp.where(col <= row, mh[size:], 0.0) for mh in m]
    mv = [_mm(tops[j], v0[i]) for j, (i, h) in enumerate(heads)]
    x = [jnp.concatenate([it["at"], jnp.where(head0, mv[2 * i], mv[2 * i + 1])], axis=1)
         for i, it in enumerate(items)]
    tx = _solve_unit_lower([t[:, :HEAD_DIM] for t in tops], [x[i] for i, h in heads])
    wu = [jnp.where(head0_wide, tx[2 * i], tx[2 * i + 1]) for i in range(len(items))]
    rhs = [jnp.concatenate([wu[i], jnp.concatenate([zeros, it["v"]], axis=1)], axis=0)
           for i, it in enumerate(items)]
    qy = [_mm(bots[j], rhs[i]) for j, (i, h) in enumerate(heads)]
    out = []
    for i, it in enumerate(items):
        qy_i = jnp.where(head0_wide, qy[2 * i], qy[2 * i + 1])
        w, u0 = wu[i][:, :LANES], wu[i][:, LANES:]
        g_mat = (jnp.where(block_diag, _mm(it["bh"], w, _TN), 0.0)
                 + jnp.where(lane2 == row2, it["g_last"], 0.0))
        d_mat = jnp.where(block_diag,
                          _mm(jnp.concatenate([it["bh"], it["kh"]], axis=0),
                              jnp.concatenate([u0, it["v"]], axis=0), _TN), 0.0)
        out.append((it["rt"] + qy_i[:, :LANES], qy_i[:, LANES:], g_mat, d_mat))
    return out


def _rwkv_prep(p, prev_row, prm, roll):
    mu, vecs, wd_pad, wa_pad, wg, bd, ltri = prm
    size = p.shape[0]
    d_r = vecs.shape[1]
    w0, a0, k_k, k_a, r_k = [vecs[i:i + 1, :] for i in range(5)]
    first = _iota(p.shape, 0) == 0
    p_prev = jnp.where(first, prev_row, roll(p, 1, 0))
    ps = p + (p_prev - p) * mu
    r, k, v = ps[:, 0:d_r], ps[:, d_r:2 * d_r], ps[:, 2 * d_r:3 * d_r]
    lora = ps[:, 3 * d_r:3 * d_r + LANES]
    gd = ps[:, 3 * d_r + LANES:]
    z_dec = -(w0 + _mm(jnp.tanh(lora), wd_pad, pa=2, pb=2))
    softplus = jnp.maximum(z_dec, 0.0) + jnp.log(1.0 + jnp.exp(-jnp.abs(z_dec)))
    lw = -jnp.exp(-softplus - 0.5)
    a = _sigmoid(a0 + _mm(lora, wa_pad, pa=2, pb=2))
    g = _mm(_sigmoid(gd), wg, pa=2, pb=2)
    kk = k * k_k
    kk = kk / jnp.maximum(jnp.sqrt(_mm(kk * kk, bd, pa=2)), 1e-12)
    k2 = k * (1.0 + (a - 1.0) * k_a)
    bv = kk * a
    bonus = _mm(r * k2 * r_k, bd, pa=2) * v
    c = _mm(ltri, lw, pb=3)
    c_last = c[size - 1:size, :]
    e_neg = jnp.exp(-c)
    e_tail = jnp.exp(c_last - c)
    full = dict(at=-kk * jnp.exp(c - lw), rt=r * jnp.exp(c), bc=bv * e_neg, kc=k2 * e_neg,
                bh=bv * e_tail, kh=k2 * e_tail, v=v, g_last=jnp.exp(c_last))
    items = [{name: val[:, pr * LANES:(pr + 1) * LANES] for name, val in full.items()}
             for pr in range(d_r // LANES)]
    return items, bonus, g


def _rwkv_finish(ys, bonus, g, prm):
    vecs, bd = prm[1], prm[5]
    lnx_g, lnx_b = vecs[5:6, :], vecs[6:7, :]
    y = jnp.concatenate(ys, axis=1)
    inv = 1.0 / HEAD_DIM
    mean = _mm(y, bd, pa=2) * inv
    dlt = y - mean
    var = _mm(dlt * dlt, bd, pa=2) * inv
    y = dlt * lax.rsqrt(var + LNX_EPS) * lnx_g + lnx_b
    return (y + bonus) * g


def _rwkv_chunks(ps, prev_row, zs, prm, roll):
    n_pairs = len(zs)
    preps = []
    for p in ps:
        preps.append(_rwkv_prep(p, prev_row, prm, roll))
        prev_row = p[p.shape[0] - 1:, :]
    local = _rwkv_local([it for items, _, _ in preps for it in items])
    outs = []
    for ci, (_, bonus, g) in enumerate(preps):
        ys = []
        zs_new = []
        for pr in range(n_pairs):
            q, y0, g_mat, d_mat = local[ci * n_pairs + pr]
            ys.append(_mm(q, zs[pr], pa=STATE_TERMS[0], pb=STATE_TERMS[1]) + y0)
            zs_new.append(_mm(g_mat, zs[pr], pa=STATE_TERMS[0], pb=STATE_TERMS[1]) + d_mat)
        zs = zs_new
        outs.append(_rwkv_finish(ys, bonus, g, prm))
    return outs, zs, prev_row


def _rwkv_chunk(p, prev_row, zs, prm, roll):
    outs, zs, _ = _rwkv_chunks([p], prev_row, zs, prm, roll)
    return outs[0], zs


def _rwkv_body(p_ref, mu_ref, vec_ref, wd_ref, wa_ref, wg_ref, bd_ref, lt_ref, o_ref, z_ref, carry_ref):
    @pl.when(pl.program_id(1) == 0)
    def _():
        z_ref[...] = jnp.zeros_like(z_ref)
        carry_ref[...] = jnp.zeros_like(carry_ref)

    prm = (mu_ref[...], vec_ref[...], wd_ref[...], wa_ref[...], wg_ref[...], bd_ref[...], lt_ref[...])
    n_pairs = z_ref.shape[0]
    n_chunks = p_ref.shape[1] // CHUNK
    ps = [p_ref[0, ci * CHUNK:(ci + 1) * CHUNK, :] for ci in range(n_chunks)]
    outs, zs, prev_row = _rwkv_chunks(ps, carry_ref[...], [z_ref[i] for i in range(n_pairs)], prm,
                                      pltpu.roll)
    for ci in range(n_chunks):
        o_ref[0, ci * CHUNK:(ci + 1) * CHUNK, :] = outs[ci].astype(o_ref.dtype)
    for i in range(n_pairs):
        z_ref[i] = zs[i]
    carry_ref[...] = prev_row


def _rwkv_call(p, mu, vecs, wd_pad, wa_pad, wg, chunks_per_step=2):
    bsz, seq, n_cols = p.shape
    d_r = vecs.shape[1]
    ts = CHUNK * chunks_per_step
    head = np.arange(d_r) // HEAD_DIM
    bd = jnp.asarray(head[:, None] == head[None, :], BF16)
    ltri = jnp.asarray(np.tril(np.ones((CHUNK, CHUNK))), BF16)
    full = lambda a: pl.BlockSpec(a.shape, lambda b, i: (0,) * a.ndim)
    consts = (mu, vecs, wd_pad, wa_pad, wg, bd, ltri)
    return pl.pallas_call(
        _rwkv_body,
        out_shape=jax.ShapeDtypeStruct((bsz, seq, d_r), BF16),
        grid=(bsz, seq // ts),
        in_specs=[pl.BlockSpec((1, ts, n_cols), lambda b, i: (b, i, 0))] + [full(a) for a in consts],
        out_specs=pl.BlockSpec((1, ts, d_r), lambda b, i: (b, i, 0)),
        scratch_shapes=[pltpu.VMEM((d_r // LANES, LANES, LANES), F32), pltpu.VMEM((1, n_cols), F32)],
        compiler_params=pltpu.CompilerParams(dimension_semantics=("arbitrary", "arbitrary"),
                                             vmem_limit_bytes=VMEM_LIMIT),
        name="rwkv7_chunked",
    )(p, *consts)


def _head_rms(x, gain, bd, eps):
    xf = x.astype(F32)
    ms = _mm(xf * xf, bd) * (1.0 / HEAD_DIM)
    return xf * lax.rsqrt(ms + eps) * gain


def _attn_body(q_ref, kp_ref, kc_ref, vp_ref, vc_ref, bias_ref, qg_ref, kg_ref, bd_ref, o_ref,
               qn_ref, kn_ref, vn_ref):
    j = pl.program_id(1)
    tq = q_ref.shape[1]
    d_a = q_ref.shape[2]
    bd = bd_ref[...]
    qn_ref[...] = (_head_rms(q_ref[0], qg_ref[...], bd, QK_EPS) * (HEAD_DIM ** -0.5)).astype(BF16)
    kn_ref[0:tq, :] = _head_rms(kp_ref[0], kg_ref[...], bd, QK_EPS).astype(BF16)
    kn_ref[tq:, :] = _head_rms(kc_ref[0], kg_ref[...], bd, QK_EPS).astype(BF16)
    vn_ref[0:tq, :] = vp_ref[0]
    vn_ref[tq:, :] = vc_ref[0]
    lane = _iota((CHUNK, LANES), 1)
    head0 = lane < HEAD_DIM
    key = _iota((CHUNK, BAND), 1)
    first_tile = j == 0
    left = tq - N_LEFT_CHUNKS * CHUNK

    def chunk_step(ci, carry):
        q0 = pl.multiple_of(ci * CHUNK, CHUNK)
        k0 = pl.multiple_of(ci * CHUNK + left, CHUNK)
        invalid = jnp.logical_and(first_tile, key < (N_LEFT_CHUNKS - ci) * CHUNK)
        for pr in range(d_a // LANES):
            sl = slice(pr * LANES, (pr + 1) * LANES)
            qp = qn_ref[pl.ds(q0, CHUNK), sl]
            kb = kn_ref[pl.ds(k0, BAND), sl]
            vb = vn_ref[pl.ds(k0, BAND), sl]
            outs = []
            for h in range(2):
                qh = jnp.where(head0 if h == 0 else jnp.logical_not(head0), qp, jnp.zeros_like(qp))
                s = _mm(qh, kb, _NT) + bias_ref[2 * pr + h]
                s = jnp.where(invalid, -1e30, s)
                e = jnp.exp(s - jnp.max(s, axis=-1, keepdims=True))
                o = _mm(e.astype(BF16), vb)
                outs.append(o / jnp.sum(e, axis=-1, keepdims=True))
            o_ref[0, pl.ds(q0, CHUNK), sl] = jnp.where(head0, outs[0], outs[1]).astype(o_ref.dtype)
        return carry

    lax.fori_loop(0, tq // CHUNK, chunk_step, 0)


def _attn_call(qkv, bias, q_norm_g, k_norm_g, tq=512):
    bsz, seq, three_d = qkv.shape
    d_a = three_d // 3
    n_heads = d_a // HEAD_DIM
    assert tq >= N_LEFT_CHUNKS * CHUNK and seq % tq == 0
    head = np.arange(d_a) // HEAD_DIM
    bd = jnp.asarray(head[:, None] == head[None, :], BF16)
    cur = lambda col: pl.BlockSpec((1, tq, d_a), lambda b, i: (b, i, col))
    prev = lambda col: pl.BlockSpec((1, tq, d_a), lambda b, i: (b, jnp.maximum(i - 1, 0), col))
    full = lambda a: pl.BlockSpec(a.shape, lambda b, i: (0,) * a.ndim)
    qg = jnp.tile(q_norm_g.astype(F32), n_heads).reshape(1, d_a)
    kg = jnp.tile(k_norm_g.astype(F32), n_heads).reshape(1, d_a)
    return pl.pallas_call(
        _attn_body,
        out_shape=jax.ShapeDtypeStruct((bsz, seq, d_a), BF16),
        grid=(bsz, seq // tq),
        in_specs=[cur(0), prev(1), cur(1), prev(2), cur(2), full(bias), full(qg), full(kg), full(bd)],
        out_specs=pl.BlockSpec((1, tq, d_a), lambda b, i: (b, i, 0)),
        scratch_shapes=[pltpu.VMEM((tq, d_a), BF16), pltpu.VMEM((2 * tq, d_a), BF16),
                        pltpu.VMEM((2 * tq, d_a), BF16)],
        compiler_params=pltpu.CompilerParams(dimension_semantics=("parallel", "parallel"),
                                             vmem_limit_bytes=VMEM_LIMIT),
        name="chunk_band_attn",
    )(qkv, qkv, qkv, qkv, qkv, bias, qg, kg, bd)


def _rel_bias_table(rel_bias):
    far = N_LEFT_CHUNKS * CHUNK + CHUNK - 1 - REL_CLIP
    near = rel_bias[:, REL_CLIP - (CHUNK - 1):2 * REL_CLIP]
    ext = jnp.concatenate([jnp.repeat(rel_bias[:, 2 * REL_CLIP:], far + 1, axis=1),
                           near[:, ::-1]], axis=1)
    rows = [ext[:, CHUNK - 1 - i:CHUNK - 1 - i + BAND] for i in range(CHUNK)]
    return jnp.stack(rows, axis=1).astype(F32)


def kernel(x, c, w_ada, b_ada, norm1_g, ffn1_w1, ffn1_w3, ffn1_w2, norm2_g, w_in, mu_shift, w0, w_decay_up, a0, w_a_up, w_g_up, k_k, k_a, r_k, lnx_g, lnx_b, q_norm_g, k_norm_g, rel_bias, w_out, norm3_g, ffn2_w1, ffn2_w3, ffn2_w2):
    bsz, seq, d = x.shape
    depth = w_ada.shape[0]
    d_r = w0.shape[1]
    n_lora_w, n_lora_a = w_decay_up.shape[1], w_a_up.shape[1]
    assert n_lora_w + n_lora_a == LANES and w_g_up.shape[1] == LANES
    n_rwkv = 3 * d_r + n_lora_w + n_lora_a + w_g_up.shape[1]
    bf = lambda w: w.astype(BF16)
    h = x
    for l in range(depth):
        mod = _mod_call(c, w_ada[l], b_ada[l]).reshape(bsz, N_MOD, d)
        h = _ffn_call(h, mod, norm1_g[l], bf(ffn1_w1[l]), bf(ffn1_w3[l]), bf(ffn1_w2[l]))
        p, qkv = _proj_call(h, mod, norm2_g[l], bf(w_in[l][:, :n_rwkv]), bf(w_in[l][:, n_rwkv:]))
        zeros_w = jnp.zeros((n_lora_a, d_r), F32)
        zeros_a = jnp.zeros((n_lora_w, d_r), F32)
        vecs = jnp.stack([w0[l], a0[l], k_k[l], k_a[l], r_k[l].reshape(-1), lnx_g[l], lnx_b[l],
                          jnp.zeros_like(w0[l])]).astype(F32)
        y_r = _rwkv_call(p, mu_shift[l].reshape(1, -1), vecs,
                         jnp.concatenate([w_decay_up[l], zeros_w], axis=0),
                         jnp.concatenate([zeros_a, w_a_up[l]], axis=0), w_g_up[l])
        y_a = _attn_call(qkv, _rel_bias_table(rel_bias[l]), q_norm_g[l], k_norm_g[l])
        h = _ffn_call(h, mod, norm3_g[l], bf(ffn2_w1[l]), bf(ffn2_w3[l]), bf(ffn2_w2[l]),
                      premix=(y_r, y_a, bf(w_out[l])))
    return h
```

```python
import functools

import numpy as np
import jax
import jax.numpy as jnp
from jax import lax
from jax.experimental import pallas as pl
from jax.experimental.pallas import tpu as pltpu

F32 = jnp.float32
BF16 = jnp.bfloat16

HEAD_DIM = 64
CHUNK = 64
N_LEFT_CHUNKS = 8
BAND = (N_LEFT_CHUNKS + 1) * CHUNK
REL_CLIP = 256
N_MOD = 9
NORM_EPS = 1e-6
QK_EPS = 1e-6
LNX_EPS = 64e-5
LANES = 128
VMEM_LIMIT = 56 * 1024 * 1024

_NN = (((1,), (0,)), ((), ()))
_NT = (((1,), (1,)), ((), ()))
_TN = (((0,), (0,)), ((), ()))


def _split(x, n):
    if x.dtype == BF16:
        return [x]
    parts = []
    r = x
    for i in range(n):
        h = r.astype(BF16)
        parts.append(h)
        if i + 1 < n:
            r = r - h.astype(F32)
    return parts


def _mm(a, b, dims=_NN, pa=1, pb=1):
    ap, bp = _split(a, pa), _split(b, pb)
    order = max(len(ap), len(bp))
    acc = None
    for i, ai in enumerate(ap):
        for j, bj in enumerate(bp):
            if i + j < order:
                t = lax.dot_general(ai, bj, dims, preferred_element_type=F32)
                acc = t if acc is None else acc + t
    return acc


def _sigmoid(x):
    return 1.0 / (1.0 + jnp.exp(-x))


def _iota(shape, axis):
    return lax.broadcasted_iota(jnp.int32, shape, axis)


def _mod_body(c_ref, w_ref, b_ref, o_ref):
    c = c_ref[...]
    s = c * _sigmoid(c)
    o_ref[...] = _mm(s, w_ref[...], pa=2, pb=2) + b_ref[...]


def _mod_call(c, w_ada, b_ada):
    bsz, d = c.shape
    n = w_ada.shape[1]
    tn = 1024
    return pl.pallas_call(
        _mod_body,
        out_shape=jax.ShapeDtypeStruct((bsz, n), F32),
        grid=(n // tn,),
        in_specs=[pl.BlockSpec((bsz, d), lambda j: (0, 0)),
                  pl.BlockSpec((d, tn), lambda j: (0, j)),
                  pl.BlockSpec((1, tn), lambda j: (0, j))],
        out_specs=pl.BlockSpec((bsz, tn), lambda j: (0, j)),
        compiler_params=pltpu.CompilerParams(dimension_semantics=("arbitrary",),
                                             vmem_limit_bytes=VMEM_LIMIT),
        name="adaln_mod",
    )(c, w_ada, b_ada.reshape(1, n))


def _norm_mod(h, ng, sh, sc):
    y = h * lax.rsqrt(jnp.mean(h * h, axis=-1, keepdims=True) + NORM_EPS)
    return (y * ng) * (1.0 + sc) + sh


def _ffn_body(premix, ff_tile, *refs):
    if premix:
        h_ref, yr_ref, ya_ref, mod_ref, ng_ref, wo_ref, w1_ref, w3_ref, w2_ref, o_ref, act_ref = refs
    else:
        h_ref, mod_ref, ng_ref, w1_ref, w3_ref, w2_ref, o_ref, act_ref = refs
    h = h_ref[0]
    mod = mod_ref[0]
    if premix:
        dr = yr_ref.shape[-1]
        mixed = (_mm(yr_ref[0], wo_ref[0:dr, :]) + _mm(ya_ref[0], wo_ref[dr:, :]))
        h = h + mod[5:6, :] * mixed
        sh, sc, g = mod[6:7, :], mod[7:8, :], mod[8:9, :]
    else:
        sh, sc, g = mod[0:1, :], mod[1:2, :], mod[2:3, :]
    n = _norm_mod(h, ng_ref[...], sh, sc).astype(BF16)
    d_ff = w1_ref.shape[1]
    for j in range(d_ff // ff_tile):
        js = slice(j * ff_tile, (j + 1) * ff_tile)
        a = _mm(n, w1_ref[:, js])
        b = _mm(n, w3_ref[:, js])
        act_ref[:, js] = (a * _sigmoid(a) * b).astype(BF16)
    o_ref[0] = h + (0.5 * g) * _mm(act_ref[...], w2_ref[...])


def _resident(shape):
    return pl.BlockSpec(shape, lambda b, i: (0,) * len(shape), pipeline_mode=pl.Buffered(1))


def _ffn_call(h, mod, norm_g, w1, w3, w2, premix=None, tm=512, ff_tile=256):
    bsz, seq, d = h.shape
    d_ff = w1.shape[1]
    tile = lambda width: pl.BlockSpec((1, tm, width), lambda b, i: (b, i, 0))
    mod_spec = pl.BlockSpec((1, N_MOD, d), lambda b, i: (b, 0, 0))
    in_specs = [tile(d)]
    args = [h]
    if premix is not None:
        y_r, y_a, w_out = premix
        in_specs += [tile(y_r.shape[-1]), tile(y_a.shape[-1])]
        args += [y_r, y_a]
    in_specs += [mod_spec, _resident((1, d))]
    args += [mod, norm_g.reshape(1, d)]
    if premix is not None:
        in_specs.append(_resident(w_out.shape))
        args.append(w_out)
    in_specs += [_resident(w1.shape), _resident(w3.shape), _resident(w2.shape)]
    args += [w1, w3, w2]
    return pl.pallas_call(
        functools.partial(_ffn_body, premix is not None, ff_tile),
        out_shape=jax.ShapeDtypeStruct((bsz, seq, d), F32),
        grid=(bsz, seq // tm),
        in_specs=in_specs,
        out_specs=tile(d),
        scratch_shapes=[pltpu.VMEM((tm, d_ff), BF16)],
        compiler_params=pltpu.CompilerParams(dimension_semantics=("parallel", "parallel"),
                                             vmem_limit_bytes=VMEM_LIMIT),
        name="ffn_premix" if premix is not None else "ffn",
    )(*args)


def _head_rms(x, gain, bd, eps):
    ms = _mm(x * x, bd) * (1.0 / HEAD_DIM)
    return x * lax.rsqrt(ms + eps) * gain


def _proj_body(h_ref, mod_ref, ng_ref, wr_ref, wa_ref, qg_ref, kg_ref, bd_ref, p_ref, qkv_ref):
    mod = mod_ref[0]
    n = _norm_mod(h_ref[0], ng_ref[...], mod[3:4, :], mod[4:5, :]).astype(BF16)
    p_ref[0] = _mm(n, wr_ref[...])
    d_a = qg_ref.shape[1]
    bd = bd_ref[...]
    q = _head_rms(_mm(n, wa_ref[:, 0:d_a]), qg_ref[...], bd, QK_EPS) * (HEAD_DIM ** -0.5)
    qkv_ref[0, :, 0:d_a] = q.astype(BF16)
    k = _head_rms(_mm(n, wa_ref[:, d_a:2 * d_a]), kg_ref[...], bd, QK_EPS)
    qkv_ref[0, :, d_a:2 * d_a] = k.astype(BF16)
    qkv_ref[0, :, 2 * d_a:] = _mm(n, wa_ref[:, 2 * d_a:]).astype(BF16)


def _proj_call(h, mod, norm_g, w_r, w_a, q_norm_g, k_norm_g, tm=512):
    bsz, seq, d = h.shape
    d_a = w_a.shape[1] // 3
    n_heads = d_a // HEAD_DIM
    head = np.arange(d_a) // HEAD_DIM
    bd = jnp.asarray(head[:, None] == head[None, :], BF16)
    qg = jnp.tile(q_norm_g.astype(F32), n_heads).reshape(1, d_a)
    kg = jnp.tile(k_norm_g.astype(F32), n_heads).reshape(1, d_a)
    tile = lambda width: pl.BlockSpec((1, tm, width), lambda b, i: (b, i, 0))
    return pl.pallas_call(
        _proj_body,
        out_shape=(jax.ShapeDtypeStruct((bsz, seq, w_r.shape[1]), F32),
                   jax.ShapeDtypeStruct((bsz, seq, w_a.shape[1]), BF16)),
        grid=(bsz, seq // tm),
        in_specs=[tile(d), pl.BlockSpec((1, N_MOD, d), lambda b, i: (b, 0, 0)), _resident((1, d)),
                  _resident(w_r.shape), _resident(w_a.shape), _resident((1, d_a)),
                  _resident((1, d_a)), _resident(bd.shape)],
        out_specs=(tile(w_r.shape[1]), tile(w_a.shape[1])),
        compiler_params=pltpu.CompilerParams(dimension_semantics=("parallel", "parallel"),
                                             vmem_limit_bytes=VMEM_LIMIT),
        name="in_proj",
    )(h, mod, norm_g.reshape(1, d), w_r, w_a, qg, kg, bd)


SOLVE_TERMS = (1, 1)
STATE_TERMS = (2, 2)


def _solve_unit_lower(n_mats, xs):
    size = n_mats[0].shape[0]
    r, c = _iota((size, size), 0), _iota((size, size), 1)
    same_block = (r // 16) == (c // 16)
    eye = jnp.where(r == c, 1.0, 0.0).astype(F32)
    mm = functools.partial(_mm, pa=SOLVE_TERMS[0], pb=SOLVE_TERMS[1])
    nd = [jnp.where(same_block, n, 0.0) for n in n_mats]
    no = [n - d for n, d in zip(n_mats, nd)]
    inv = [eye + d for d in nd]
    pw = nd
    for _ in range(3):
        pw = [mm(w, w) for w in pw]
        inv = [t + mm(w, t) for w, t in zip(pw, inv)]
    p = [mm(t, o) for t, o in zip(inv, no)]
    tx = [mm(t, x) for t, x in zip(inv, xs)]
    p2 = [mm(q, q) for q in p]
    tx = [t + mm(q, t) for q, t in zip(p2, tx)]
    return [t + mm(q, t) for q, t in zip(p, tx)]


def _rwkv_local(items):
    size = items[0]["at"].shape[0]
    lane = _iota((size, LANES), 1)
    row = _iota((size, LANES), 0)
    col = lane % HEAD_DIM
    head0 = lane < HEAD_DIM
    head0_wide = jnp.concatenate([head0, head0], axis=1)
    lane2 = _iota((LANES, LANES), 1)
    row2 = _iota((LANES, LANES), 0)
    in_head = (lane2 < HEAD_DIM, lane2 >= HEAD_DIM)
    block_diag = (lane2 < HEAD_DIM) == (row2 < HEAD_DIM)
    zeros = jnp.zeros((size, LANES), F32)
    heads = [(i, h) for i in range(len(items)) for h in range(2)]

    ar = [jnp.concatenate([it["at"], it["rt"]], axis=0) for it in items]
    bk = [jnp.concatenate([it["bc"], it["kc"]], axis=0) for it in items]
    v0 = [jnp.concatenate([zeros, it["v"]], axis=0) for it in items]
    m = [_mm(jnp.where(in_head[h], ar[i], 0.0), bk[i], _NT) for i, h in heads]
    tops = [jnp.where(col < row, mh[:size], 0.0) for mh in m]
    bots = [jnp.where(col <= row, mh[size:], 0.0) for mh in m]
    mv = [_mm(tops[j], v0[i]) for j, (i, h) in enumerate(heads)]
    x = [jnp.concatenate([it["at"], jnp.where(head0, mv[2 * i], mv[2 * i + 1])], axis=1)
         for i, it in enumerate(items)]
    tx = _solve_unit_lower([t[:, :HEAD_DIM] for t in tops], [x[i] for i, h in heads])
    wu = [jnp.where(head0_wide, tx[2 * i], tx[2 * i + 1]) for i in range(len(items))]
    rhs = [jnp.concatenate([wu[i], jnp.concatenate([zeros, it["v"]], axis=1)], axis=0)
           for i, it in enumerate(items)]
    qy = [_mm(bots[j], rhs[i]) for j, (i, h) in enumerate(heads)]
    out = []
    for i, it in enumerate(items):
        qy_i = jnp.where(head0_wide, qy[2 * i], qy[2 * i + 1])
        w, u0 = wu[i][:, :LANES], wu[i][:, LANES:]
        g_mat = (jnp.where(block_diag, _mm(it["bh"], w, _TN), 0.0)
                 + jnp.where(lane2 == row2, it["g_last"], 0.0))
        d_mat = jnp.where(block_diag,
                          _mm(jnp.concatenate([it["bh"], it["kh"]], axis=0),
                              jnp.concatenate([u0, it["v"]], axis=0), _TN), 0.0)
        out.append((it["rt"] + qy_i[:, :LANES], qy_i[:, LANES:], g_mat, d_mat))
    return out


def _rwkv_prep(p, prev_row, prm, roll):
    mu, vecs, wd_pad, wa_pad, wg, bd, ltri = prm
    size = p.shape[0]
    d_r = vecs.shape[1]
    w0, a0, k_k, k_a, r_k = [vecs[i:i + 1, :] for i in range(5)]
    first = _iota(p.shape, 0) == 0
    p_prev = jnp.where(first, prev_row, roll(p, 1, 0))
    ps = p + (p_prev - p) * mu
    r, k, v = ps[:, 0:d_r], ps[:, d_r:2 * d_r], ps[:, 2 * d_r:3 * d_r]
    lora = ps[:, 3 * d_r:3 * d_r + LANES]
    gd = ps[:, 3 * d_r + LANES:]
    z_dec = -(w0 + _mm(jnp.tanh(lora), wd_pad, pa=2, pb=2))
    softplus = jnp.maximum(z_dec, 0.0) + jnp.log(1.0 + jnp.exp(-jnp.abs(z_dec)))
    lw = -jnp.exp(-softplus - 0.5)
    a = _sigmoid(a0 + _mm(lora, wa_pad, pa=2, pb=2))
    g = _mm(_sigmoid(gd), wg, pa=2, pb=2)
    kk = k * k_k
    kk = kk / jnp.maximum(jnp.sqrt(_mm(kk * kk, bd, pa=2)), 1e-12)
    k2 = k * (1.0 + (a - 1.0) * k_a)
    bv = kk * a
    bonus = _mm(r * k2 * r_k, bd, pa=2) * v
    c = _mm(ltri, lw, pb=3)
    c_last = c[size - 1:size, :]
    e_neg = jnp.exp(-c)
    e_tail = jnp.exp(c_last - c)
    full = dict(at=-kk * jnp.exp(c - lw), rt=r * jnp.exp(c), bc=bv * e_neg, kc=k2 * e_neg,
                bh=bv * e_tail, kh=k2 * e_tail, v=v, g_last=jnp.exp(c_last))
    items = [{name: val[:, pr * LANES:(pr + 1) * LANES] for name, val in full.items()}
             for pr in range(d_r // LANES)]
    return items, bonus, g


def _rwkv_finish(ys, bonus, g, prm):
    vecs, bd = prm[1], prm[5]
    lnx_g, lnx_b = vecs[5:6, :], vecs[6:7, :]
    y = jnp.concatenate(ys, axis=1)
    inv = 1.0 / HEAD_DIM
    mean = _mm(y, bd, pa=2) * inv
    dlt = y - mean
    var = _mm(dlt * dlt, bd, pa=2) * inv
    y = dlt * lax.rsqrt(var + LNX_EPS) * lnx_g + lnx_b
    return (y + bonus) * g


def _rwkv_chunks(ps, prev_row, zs, prm, roll):
    n_pairs = len(zs)
    preps = []
    for p in ps:
        preps.append(_rwkv_prep(p, prev_row, prm, roll))
        prev_row = p[p.shape[0] - 1:, :]
    local = _rwkv_local([it for items, _, _ in preps for it in items])
    outs = []
    for ci, (_, bonus, g) in enumerate(preps):
        ys = []
        zs_new = []
        for pr in range(n_pairs):
            q, y0, g_mat, d_mat = local[ci * n_pairs + pr]
            ys.append(_mm(q, zs[pr], pa=STATE_TERMS[0], pb=STATE_TERMS[1]) + y0)
            zs_new.append(_mm(g_mat, zs[pr], pa=STATE_TERMS[0], pb=STATE_TERMS[1]) + d_mat)
        zs = zs_new
        outs.append(_rwkv_finish(ys, bonus, g, prm))
    return outs, zs, prev_row


def _rwkv_chunk(p, prev_row, zs, prm, roll):
    outs, zs, _ = _rwkv_chunks([p], prev_row, zs, prm, roll)
    return outs[0], zs


def _rwkv_body(p_ref, mu_ref, vec_ref, wd_ref, wa_ref, wg_ref, bd_ref, lt_ref, o_ref, z_ref, carry_ref):
    @pl.when(pl.program_id(1) == 0)
    def _():
        z_ref[...] = jnp.zeros_like(z_ref)
        carry_ref[...] = jnp.zeros_like(carry_ref)

    prm = (mu_ref[...], vec_ref[...], wd_ref[...], wa_ref[...], wg_ref[...], bd_ref[...], lt_ref[...])
    n_pairs = z_ref.shape[0]
    n_chunks = p_ref.shape[1] // CHUNK
    ps = [p_ref[0, ci * CHUNK:(ci + 1) * CHUNK, :] for ci in range(n_chunks)]
    outs, zs, prev_row = _rwkv_chunks(ps, carry_ref[...], [z_ref[i] for i in range(n_pairs)], prm,
                                      pltpu.roll)
    for ci in range(n_chunks):
        o_ref[0, ci * CHUNK:(ci + 1) * CHUNK, :] = outs[ci].astype(o_ref.dtype)
    for i in range(n_pairs):
        z_ref[i] = zs[i]
    carry_ref[...] = prev_row


def _rwkv_call(p, mu, vecs, wd_pad, wa_pad, wg, chunks_per_step=2):
    bsz, seq, n_cols = p.shape
    d_r = vecs.shape[1]
    ts = CHUNK * chunks_per_step
    head = np.arange(d_r) // HEAD_DIM
    bd = jnp.asarray(head[:, None] == head[None, :], BF16)
    ltri = jnp.asarray(np.tril(np.ones((CHUNK, CHUNK))), BF16)
    full = lambda a: pl.BlockSpec(a.shape, lambda b, i: (0,) * a.ndim)
    consts = (mu, vecs, wd_pad, wa_pad, wg, bd, ltri)
    return pl.pallas_call(
        _rwkv_body,
        out_shape=jax.ShapeDtypeStruct((bsz, seq, d_r), BF16),
        grid=(bsz, seq // ts),
        in_specs=[pl.BlockSpec((1, ts, n_cols), lambda b, i: (b, i, 0))] + [full(a) for a in consts],
        out_specs=pl.BlockSpec((1, ts, d_r), lambda b, i: (b, i, 0)),
        scratch_shapes=[pltpu.VMEM((d_r // LANES, LANES, LANES), F32), pltpu.VMEM((1, n_cols), F32)],
        compiler_params=pltpu.CompilerParams(dimension_semantics=("arbitrary", "arbitrary"),
                                             vmem_limit_bytes=VMEM_LIMIT),
        name="rwkv7_chunked",
    )(p, *consts)


def _attn_body(q_ref, kp_ref, kc_ref, vp_ref, vc_ref, bias_ref, o_ref, kn_ref, vn_ref):
    j = pl.program_id(1)
    tq = q_ref.shape[1]
    d_a = q_ref.shape[2]
    kn_ref[0:tq, :] = kp_ref[0]
    kn_ref[tq:, :] = kc_ref[0]
    vn_ref[0:tq, :] = vp_ref[0]
    vn_ref[tq:, :] = vc_ref[0]
    lane = _iota((CHUNK, LANES), 1)
    head0 = lane < HEAD_DIM
    key = _iota((CHUNK, BAND), 1)
    first_tile = j == 0
    left = tq - N_LEFT_CHUNKS * CHUNK

    def scores(ci, pr):
        sl = slice(pr * LANES, (pr + 1) * LANES)
        qp = q_ref[0, ci * CHUNK:(ci + 1) * CHUNK, sl]
        kb = kn_ref[ci * CHUNK + left:ci * CHUNK + left + BAND, sl]
        invalid = jnp.logical_and(first_tile, key < (N_LEFT_CHUNKS - ci) * CHUNK)
        es = []
        for h in range(2):
            qh = jnp.where(head0 if h == 0 else jnp.logical_not(head0), qp, jnp.zeros_like(qp))
            s = _mm(qh, kb, _NT) + bias_ref[2 * pr + h]
            if ci < N_LEFT_CHUNKS:
                s = jnp.where(invalid, -1e30, s)
            e = jnp.exp(s - jnp.max(s, axis=-1, keepdims=True))
            es.append((e.astype(BF16), jnp.sum(e, axis=-1, keepdims=True)))
        return es

    def weighted_values(ci, pr, es):
        sl = slice(pr * LANES, (pr + 1) * LANES)
        vb = vn_ref[ci * CHUNK + left:ci * CHUNK + left + BAND, sl]
        outs = [_mm(e, vb) / l for e, l in es]
        o_ref[0, ci * CHUNK:(ci + 1) * CHUNK, sl] = jnp.where(head0, outs[0], outs[1]).astype(o_ref.dtype)

    units = [(ci, pr) for ci in range(tq // CHUNK) for pr in range(d_a // LANES)]
    pending = None
    for unit in units:
        es = scores(*unit)
        if pending is not None:
            weighted_values(*pending)
        pending = (*unit, es)
    weighted_values(*pending)


def _attn_call(qkv, bias, tq=512):
    bsz, seq, three_d = qkv.shape
    d_a = three_d // 3
    assert tq >= N_LEFT_CHUNKS * CHUNK and seq % tq == 0
    cur = lambda col: pl.BlockSpec((1, tq, d_a), lambda b, i: (b, i, col))
    prev = lambda col: pl.BlockSpec((1, tq, d_a), lambda b, i: (b, jnp.maximum(i - 1, 0), col))
    full = lambda a: pl.BlockSpec(a.shape, lambda b, i: (0,) * a.ndim)
    return pl.pallas_call(
        _attn_body,
        out_shape=jax.ShapeDtypeStruct((bsz, seq, d_a), BF16),
        grid=(bsz, seq // tq),
        in_specs=[cur(0), prev(1), cur(1), prev(2), cur(2), full(bias)],
        out_specs=pl.BlockSpec((1, tq, d_a), lambda b, i: (b, i, 0)),
        scratch_shapes=[pltpu.VMEM((2 * tq, d_a), BF16), pltpu.VMEM((2 * tq, d_a), BF16)],
        compiler_params=pltpu.CompilerParams(dimension_semantics=("parallel", "parallel"),
                                             vmem_limit_bytes=VMEM_LIMIT),
        name="chunk_band_attn",
    )(qkv, qkv, qkv, qkv, qkv, bias)


def _rel_bias_table(rel_bias):
    far = N_LEFT_CHUNKS * CHUNK + CHUNK - 1 - REL_CLIP
    near = rel_bias[:, REL_CLIP - (CHUNK - 1):2 * REL_CLIP]
    ext = jnp.concatenate([jnp.repeat(rel_bias[:, 2 * REL_CLIP:], far + 1, axis=1),
                           near[:, ::-1]], axis=1)
    rows = [ext[:, CHUNK - 1 - i:CHUNK - 1 - i + BAND] for i in range(CHUNK)]
    return jnp.stack(rows, axis=1).astype(F32)


def kernel(x, c, w_ada, b_ada, norm1_g, ffn1_w1, ffn1_w3, ffn1_w2, norm2_g, w_in, mu_shift, w0, w_decay_up, a0, w_a_up, w_g_up, k_k, k_a, r_k, lnx_g, lnx_b, q_norm_g, k_norm_g, rel_bias, w_out, norm3_g, ffn2_w1, ffn2_w3, ffn2_w2):
    bsz, seq, d = x.shape
    depth = w_ada.shape[0]
    d_r = w0.shape[1]
    n_lora_w, n_lora_a = w_decay_up.shape[1], w_a_up.shape[1]
    assert n_lora_w + n_lora_a == LANES and w_g_up.shape[1] == LANES
    n_rwkv = 3 * d_r + n_lora_w + n_lora_a + w_g_up.shape[1]
    bf = lambda w: w.astype(BF16)
    h = x
    for l in range(depth):
        mod = _mod_call(c, w_ada[l], b_ada[l]).reshape(bsz, N_MOD, d)
        h = _ffn_call(h, mod, norm1_g[l], bf(ffn1_w1[l]), bf(ffn1_w3[l]), bf(ffn1_w2[l]))
        p, qkv = _proj_call(h, mod, norm2_g[l], bf(w_in[l][:, :n_rwkv]), bf(w_in[l][:, n_rwkv:]),
                            q_norm_g[l], k_norm_g[l])
        zeros_w = jnp.zeros((n_lora_a, d_r), F32)
        zeros_a = jnp.zeros((n_lora_w, d_r), F32)
        vecs = jnp.stack([w0[l], a0[l], k_k[l], k_a[l], r_k[l].reshape(-1), lnx_g[l], lnx_b[l],
                          jnp.zeros_like(w0[l])]).astype(F32)
        y_r = _rwkv_call(p, mu_shift[l].reshape(1, -1), vecs,
                         jnp.concatenate([w_decay_up[l], zeros_w], axis=0),
                         jnp.concatenate([zeros_a, w_a_up[l]], axis=0), w_g_up[l])
        y_a = _attn_call(qkv, _rel_bias_table(rel_bias[l]))
        h = _ffn_call(h, mod, norm3_g[l], bf(ffn2_w1[l]), bf(ffn2_w3[l]), bf(ffn2_w2[l]),
                      premix=(y_r, y_a, bf(w_out[l])))
    return h
```

```python
import functools

import numpy as np
import jax
import jax.numpy as jnp
from jax import lax
from jax.experimental import pallas as pl
from jax.experimental.pallas import tpu as pltpu

F32 = jnp.float32
BF16 = jnp.bfloat16

HEAD_DIM = 64
CHUNK = 64
N_LEFT_CHUNKS = 8
BAND = (N_LEFT_CHUNKS + 1) * CHUNK
REL_CLIP = 256
N_MOD = 9
NORM_EPS = 1e-6
QK_EPS = 1e-6
LNX_EPS = 64e-5
LANES = 128
VMEM_LIMIT = 56 * 1024 * 1024

_NN = (((1,), (0,)), ((), ()))
_NT = (((1,), (1,)), ((), ()))
_TN = (((0,), (0,)), ((), ()))


def _split(x, n):
    if isinstance(x, (list, tuple)):
        return list(x)
    if x.dtype == BF16:
        return [x]
    parts = []
    r = x
    for i in range(n):
        h = r.astype(BF16)
        parts.append(h)
        if i + 1 < n:
            r = r - h.astype(F32)
    return parts


def _mm(a, b, dims=_NN, pa=1, pb=1):
    ap, bp = _split(a, pa), _split(b, pb)
    order = max(len(ap), len(bp))
    acc = None
    for i, ai in enumerate(ap):
        for j, bj in enumerate(bp):
            if i + j < order:
                t = lax.dot_general(ai, bj, dims, preferred_element_type=F32)
                acc = t if acc is None else acc + t
    return acc


def _sigmoid(x):
    return 1.0 / (1.0 + jnp.exp(-x))


def _iota(shape, axis):
    return lax.broadcasted_iota(jnp.int32, shape, axis)


def _mod_body(c_ref, w_ref, b_ref, o_ref):
    c = c_ref[...]
    s = c * _sigmoid(c)
    o_ref[...] = _mm(s, w_ref[...], pa=2, pb=2) + b_ref[...]


def _mod_call(c, w_ada, b_ada):
    bsz, d = c.shape
    n = w_ada.shape[1]
    tn = 1024
    return pl.pallas_call(
        _mod_body,
        out_shape=jax.ShapeDtypeStruct((bsz, n), F32),
        grid=(n // tn,),
        in_specs=[pl.BlockSpec((bsz, d), lambda j: (0, 0)),
                  pl.BlockSpec((d, tn), lambda j: (0, j)),
                  pl.BlockSpec((1, tn), lambda j: (0, j))],
        out_specs=pl.BlockSpec((bsz, tn), lambda j: (0, j)),
        compiler_params=pltpu.CompilerParams(dimension_semantics=("arbitrary",),
                                             vmem_limit_bytes=VMEM_LIMIT),
        name="adaln_mod",
    )(c, w_ada, b_ada.reshape(1, n))


def _norm_mod(h, ng, sh, sc):
    y = h * lax.rsqrt(jnp.mean(h * h, axis=-1, keepdims=True) + NORM_EPS)
    return (y * ng) * (1.0 + sc) + sh


def _ffn_body(premix, ff_tile, *refs):
    if premix:
        h_ref, yr_ref, ya_ref, mod_ref, ng_ref, wo_ref, w1_ref, w3_ref, w2_ref, o_ref, act_ref = refs
    else:
        h_ref, mod_ref, ng_ref, w1_ref, w3_ref, w2_ref, o_ref, act_ref = refs
    h = h_ref[0]
    mod = mod_ref[0]
    if premix:
        dr = yr_ref.shape[-1]
        mixed = (_mm(yr_ref[0], wo_ref[0:dr, :]) + _mm(ya_ref[0], wo_ref[dr:, :]))
        h = h + mod[5:6, :] * mixed
        sh, sc, g = mod[6:7, :], mod[7:8, :], mod[8:9, :]
    else:
        sh, sc, g = mod[0:1, :], mod[1:2, :], mod[2:3, :]
    n = _norm_mod(h, ng_ref[...], sh, sc).astype(BF16)
    d_ff = w1_ref.shape[1]
    for j in range(d_ff // ff_tile):
        js = slice(j * ff_tile, (j + 1) * ff_tile)
        a = _mm(n, w1_ref[:, js])
        b = _mm(n, w3_ref[:, js])
        act_ref[:, js] = (a * _sigmoid(a) * b).astype(BF16)
    o_ref[0] = h + (0.5 * g) * _mm(act_ref[...], w2_ref[...])


def _resident(shape):
    return pl.BlockSpec(shape, lambda b, i: (0,) * len(shape), pipeline_mode=pl.Buffered(1))


def _ffn_call(h, mod, norm_g, w1, w3, w2, premix=None, tm=512, ff_tile=256):
    bsz, seq, d = h.shape
    d_ff = w1.shape[1]
    tile = lambda width: pl.BlockSpec((1, tm, width), lambda b, i: (b, i, 0))
    mod_spec = pl.BlockSpec((1, N_MOD, d), lambda b, i: (b, 0, 0))
    in_specs = [tile(d)]
    args = [h]
    if premix is not None:
        y_r, y_a, w_out = premix
        in_specs += [tile(y_r.shape[-1]), tile(y_a.shape[-1])]
        args += [y_r, y_a]
    in_specs += [mod_spec, _resident((1, d))]
    args += [mod, norm_g.reshape(1, d)]
    if premix is not None:
        in_specs.append(_resident(w_out.shape))
        args.append(w_out)
    in_specs += [_resident(w1.shape), _resident(w3.shape), _resident(w2.shape)]
    args += [w1, w3, w2]
    return pl.pallas_call(
        functools.partial(_ffn_body, premix is not None, ff_tile),
        out_shape=jax.ShapeDtypeStruct((bsz, seq, d), F32),
        grid=(bsz, seq // tm),
        in_specs=in_specs,
        out_specs=tile(d),
        scratch_shapes=[pltpu.VMEM((tm, d_ff), BF16)],
        compiler_params=pltpu.CompilerParams(dimension_semantics=("parallel", "parallel"),
                                             vmem_limit_bytes=VMEM_LIMIT),
        name="ffn_premix" if premix is not None else "ffn",
    )(*args)


def _head_rms(x, gain, bd, eps):
    ms = _mm(x * x, bd) * (1.0 / HEAD_DIM)
    return x * lax.rsqrt(ms + eps) * gain


def _proj_body(h_ref, mod_ref, ng_ref, wr_ref, wa_ref, qg_ref, kg_ref, bd_ref, p_ref, qkv_ref):
    mod = mod_ref[0]
    n = _norm_mod(h_ref[0], ng_ref[...], mod[3:4, :], mod[4:5, :]).astype(BF16)
    p_ref[0] = _mm(n, wr_ref[...])
    d_a = qg_ref.shape[1]
    bd = bd_ref[...]
    q = _head_rms(_mm(n, wa_ref[:, 0:d_a]), qg_ref[...], bd, QK_EPS) * (HEAD_DIM ** -0.5)
    qkv_ref[0, :, 0:d_a] = q.astype(BF16)
    k = _head_rms(_mm(n, wa_ref[:, d_a:2 * d_a]), kg_ref[...], bd, QK_EPS)
    qkv_ref[0, :, d_a:2 * d_a] = k.astype(BF16)
    qkv_ref[0, :, 2 * d_a:] = _mm(n, wa_ref[:, 2 * d_a:]).astype(BF16)


def _proj_call(h, mod, norm_g, w_r, w_a, q_norm_g, k_norm_g, tm=512):
    bsz, seq, d = h.shape
    d_a = w_a.shape[1] // 3
    n_heads = d_a // HEAD_DIM
    head = np.arange(d_a) // HEAD_DIM
    bd = jnp.asarray(head[:, None] == head[None, :], BF16)
    qg = jnp.tile(q_norm_g.astype(F32), n_heads).reshape(1, d_a)
    kg = jnp.tile(k_norm_g.astype(F32), n_heads).reshape(1, d_a)
    tile = lambda width: pl.BlockSpec((1, tm, width), lambda b, i: (b, i, 0))
    return pl.pallas_call(
        _proj_body,
        out_shape=(jax.ShapeDtypeStruct((bsz, seq, w_r.shape[1]), F32),
                   jax.ShapeDtypeStruct((bsz, seq, w_a.shape[1]), BF16)),
        grid=(bsz, seq // tm),
        in_specs=[tile(d), pl.BlockSpec((1, N_MOD, d), lambda b, i: (b, 0, 0)), _resident((1, d)),
                  _resident(w_r.shape), _resident(w_a.shape), _resident((1, d_a)),
                  _resident((1, d_a)), _resident(bd.shape)],
        out_specs=(tile(w_r.shape[1]), tile(w_a.shape[1])),
        compiler_params=pltpu.CompilerParams(dimension_semantics=("parallel", "parallel"),
                                             vmem_limit_bytes=VMEM_LIMIT),
        name="in_proj",
    )(h, mod, norm_g.reshape(1, d), w_r, w_a, qg, kg, bd)


SOLVE_TERMS = (1, 1)
STATE_TERMS = (2, 2)
LORA_TERMS = (2, 2)
SUM_TERMS = (1, 1, 2, 1)
GROUP = 256


def _block_diag(x):
    if isinstance(x, (list, tuple)):
        return [_block_diag(t) for t in x]
    n = x.shape[1] // HEAD_DIM
    tiled = jnp.concatenate([x] * n, axis=0)
    keep = (_iota(tiled.shape, 0) // HEAD_DIM) == (_iota(tiled.shape, 1) // HEAD_DIM)
    return jnp.where(keep, tiled, jnp.zeros_like(tiled))


def _diag_blocks(full):
    n = full.shape[1] // HEAD_DIM
    lane_head = _iota((HEAD_DIM, full.shape[1]), 1) // HEAD_DIM
    out = None
    for h in range(n):
        part = jnp.where(lane_head == h, full[h * HEAD_DIM:(h + 1) * HEAD_DIM, :], 0.0)
        out = part if out is None else out + part
    return out


def _mm_rows(lhs_list, rhs, pa=1, pb=1):
    rows = [l.shape[0] for l in lhs_list]
    lparts = [_split(l, pa) for l in lhs_list]
    rparts = _split(rhs, pb)
    order = max(pa, len(rparts))
    outs = [None] * len(lhs_list)
    for j, rj in enumerate(rparts):
        n_i = min(pa, order - j)
        stack = jnp.concatenate([lp[i] for i in range(n_i) for lp in lparts], axis=0)
        res = lax.dot_general(stack, rj, _NN, preferred_element_type=F32)
        off = 0
        for i in range(n_i):
            for k, nrow in enumerate(rows):
                piece = res[off:off + nrow]
                outs[k] = piece if outs[k] is None else outs[k] + piece
                off += nrow
    return outs


def _unit_lower_inverse(n_mats):
    shape = n_mats[0].shape
    r, c = _iota(shape, 0), _iota(shape, 1) % HEAD_DIM
    same_block = (r // 16) == (c // 16)
    eye = jnp.where(r == c, 1.0, 0.0).astype(F32)
    bd = lambda x: _block_diag(_split(x, SOLVE_TERMS[1]))
    mm = lambda ls, w: _mm_rows(ls, w, pa=SOLVE_TERMS[0], pb=SOLVE_TERMS[1])
    nd = [jnp.where(same_block, n, 0.0) for n in n_mats]
    no = [n - d for n, d in zip(n_mats, nd)]
    acc = [eye + d for d in nd]
    pw = [mm([d], bd(d))[0] for d in nd]
    for last in (False, False, True):
        res = [mm([t] if last else [w, t], bd(w)) for w, t in zip(pw, acc)]
        acc = [t + r_[-1] for t, r_ in zip(acc, res)]
        if not last:
            pw = [r_[0] for r_ in res]
    p = [mm([t], bd(o))[0] for t, o in zip(acc, no)]
    p2 = [mm([q], bd(q))[0] for q in p]
    acc = [t + mm([q], bd(t))[0] for q, t in zip(p, acc)]
    return [t + mm([q], bd(t))[0] for q, t in zip(p2, acc)]


def _rwkv_local(items):
    size = items[0]["at"].shape[0]
    row = _iota((size, GROUP), 0)
    col = _iota((size, GROUP), 1) % HEAD_DIM
    strict, incl = col < row, col <= row
    bd = lambda x: _block_diag(x.astype(BF16))

    m = [_mm(jnp.concatenate([it["at"], it["rt"]], axis=0),
             jnp.concatenate([bd(it["bc"]), bd(it["kc"])], axis=0), _NT) for it in items]
    n_mat = [jnp.where(strict, mi[:size, :GROUP], 0.0) for mi in m]
    mak = [jnp.where(strict, mi[:size, GROUP:], 0.0) for mi in m]
    mrb = [jnp.where(incl, mi[size:, :GROUP], 0.0) for mi in m]
    mrk = [jnp.where(incl, mi[size:, GROUP:], 0.0) for mi in m]
    mv = [_mm_rows([a, b], bd(it["v"])) for a, b, it in zip(mak, mrk, items)]
    t_inv = _unit_lower_inverse(n_mat)
    wu = [_mm(t, jnp.concatenate([bd(it["at"]), bd(x[0])], axis=1)) for t, it, x in zip(t_inv, items, mv)]
    qy = [_mm(b, jnp.concatenate([bd(x[:, :GROUP]), bd(x[:, GROUP:])], axis=1)) for b, x in zip(mrb, wu)]
    out = []
    for i, it in enumerate(items):
        w, u0 = wu[i][:, :GROUP], wu[i][:, GROUP:]
        g_mat = _diag_blocks(_mm(it["bh"], w, _TN)) + jnp.where(row == col, it["g_last"], 0.0)
        d_mat = _diag_blocks(_mm(jnp.concatenate([it["bh"], it["kh"]], axis=0),
                                 jnp.concatenate([u0, it["v"]], axis=0), _TN))
        out.append((it["rt"] + qy[i][:, :GROUP], qy[i][:, GROUP:] + mv[i][1], g_mat, d_mat))
    return out


def _head_sums(x, bd, terms=1):
    parts = [_mm(x[:, gi * GROUP:(gi + 1) * GROUP], bd, pa=terms) for gi in range(x.shape[1] // GROUP)]
    return jnp.concatenate(parts, axis=1)


def _rwkv_prep(ps, prev_rows, prm, roll):
    mu, vecs, wd_pad, wa_pad, wg, bd, ltri = prm
    size = ps[0][0].shape[0]
    d_r = vecs.shape[1]
    w0, a0, k_k, k_a, r_k = [vecs[i:i + 1, :] for i in range(5)]
    shifted, new_prev = [], []
    for b, chunks in enumerate(ps):
        p = jnp.concatenate(chunks, axis=0) if len(chunks) > 1 else chunks[0]
        first = _iota(p.shape, 0) == 0
        p_prev = jnp.where(first, prev_rows[b], roll(p, 1, 0))
        shifted.append(p + (p_prev - p) * mu)
        new_prev.append(p[p.shape[0] - 1:, :])
    x = jnp.concatenate(shifted, axis=0) if len(shifted) > 1 else shifted[0]
    r, k, v = x[:, 0:d_r], x[:, d_r:2 * d_r], x[:, 2 * d_r:3 * d_r]
    lora = x[:, 3 * d_r:3 * d_r + LANES]
    gd = x[:, 3 * d_r + LANES:]
    z_dec = -(w0 + _mm(jnp.tanh(lora), wd_pad, pa=2, pb=2))
    softplus = jnp.maximum(z_dec, 0.0) + jnp.log(1.0 + jnp.exp(-jnp.abs(z_dec)))
    lw = -jnp.exp(-softplus - 0.5)
    a = _sigmoid(a0 + _mm(lora, wa_pad, pa=LORA_TERMS[0], pb=LORA_TERMS[1]))
    g = _mm(_sigmoid(gd), wg, pa=LORA_TERMS[0], pb=LORA_TERMS[1])
    kk = k * k_k
    kk = kk / jnp.maximum(jnp.sqrt(_head_sums(kk * kk, bd, SUM_TERMS[0])), 1e-12)
    k2 = k * (1.0 + (a - 1.0) * k_a)
    bv = kk * a
    bonus = _head_sums(r * k2 * r_k, bd, SUM_TERMS[1]) * v
    out = []
    for ci in range(x.shape[0] // size):
        rows = slice(ci * size, (ci + 1) * size)
        lw_c = lw[rows]
        c = _mm(ltri, lw_c, pb=3)
        c_last = c[size - 1:size, :]
        e_neg = jnp.exp(-c)
        e_tail = jnp.exp(c_last - c)
        full = dict(at=-kk[rows] * jnp.exp(c - lw_c), rt=r[rows] * jnp.exp(c), bc=bv[rows] * e_neg,
                    kc=k2[rows] * e_neg, bh=bv[rows] * e_tail, kh=k2[rows] * e_tail, v=v[rows],
                    g_last=jnp.exp(c_last))
        items = [{name: val[:, gi * GROUP:(gi + 1) * GROUP] for name, val in full.items()}
                 for gi in range(d_r // GROUP)]
        out.append((items, bonus[rows], g[rows]))
    return out, new_prev


def _rwkv_finish(ys, bonus, g, prm):
    vecs, bd = prm[1], prm[5]
    lnx_g, lnx_b = vecs[5:6, :], vecs[6:7, :]
    inv = 1.0 / HEAD_DIM
    mean = _head_sums(ys, bd, SUM_TERMS[2]) * inv
    dlt = ys - mean
    var = _head_sums(dlt * dlt, bd, SUM_TERMS[3]) * inv
    y = dlt * lax.rsqrt(var + LNX_EPS) * lnx_g + lnx_b
    return (y + bonus) * g


def _rwkv_chunks(ps, prev_rows, zs, prm, roll):
    n_seq, n_chunks, n_groups = len(ps), len(ps[0]), len(zs[0])
    size = ps[0][0].shape[0]
    preps, prev_rows = _rwkv_prep(ps, prev_rows, prm, roll)
    local = _rwkv_local([it for items, _, _ in preps for it in items])
    zs = [list(z) for z in zs]
    outs = [[None] * n_chunks for _ in range(n_seq)]
    for ci in range(n_chunks):
        ys = [[None] * n_groups for _ in range(n_seq)]
        for b in range(n_seq):
            for gi in range(n_groups):
                q, y0, g_mat, d_mat = local[(b * n_chunks + ci) * n_groups + gi]
                z_bd = _block_diag(_split(zs[b][gi], STATE_TERMS[1]))
                qz, gz = _mm_rows([q, g_mat], z_bd, pa=STATE_TERMS[0], pb=STATE_TERMS[1])
                ys[b][gi] = qz + y0
                zs[b][gi] = gz + d_mat
        y_all = jnp.concatenate([jnp.concatenate(ys[b], axis=1) for b in range(n_seq)], axis=0)
        bonus = jnp.concatenate([preps[b * n_chunks + ci][1] for b in range(n_seq)], axis=0)
        gate = jnp.concatenate([preps[b * n_chunks + ci][2] for b in range(n_seq)], axis=0)
        o_all = _rwkv_finish(y_all, bonus, gate, prm)
        for b in range(n_seq):
            outs[b][ci] = o_all[b * size:(b + 1) * size]
    return outs, zs, prev_rows


def _rwkv_chunk(p, prev_row, zs, prm, roll):
    outs, zs, _ = _rwkv_chunks([[p]], [prev_row], [zs], prm, roll)
    return outs[0][0], zs[0]


def _rwkv_body(p_ref, mu_ref, vec_ref, wd_ref, wa_ref, wg_ref, bd_ref, lt_ref, o_ref, z_ref, carry_ref):
    @pl.when(pl.program_id(0) == 0)
    def _():
        z_ref[...] = jnp.zeros_like(z_ref)
        carry_ref[...] = jnp.zeros_like(carry_ref)

    prm = (mu_ref[...], vec_ref[...], wd_ref[...], wa_ref[...], wg_ref[...], bd_ref[...], lt_ref[...])
    n_seq, n_groups = z_ref.shape[0], z_ref.shape[1]
    n_chunks = p_ref.shape[1] // CHUNK
    ps = [[p_ref[b, ci * CHUNK:(ci + 1) * CHUNK, :] for ci in range(n_chunks)] for b in range(n_seq)]
    zs = [[z_ref[b, gi] for gi in range(n_groups)] for b in range(n_seq)]
    outs, zs, prev_rows = _rwkv_chunks(ps, [carry_ref[b] for b in range(n_seq)], zs, prm, pltpu.roll)
    for b in range(n_seq):
        for ci in range(n_chunks):
            o_ref[b, ci * CHUNK:(ci + 1) * CHUNK, :] = outs[b][ci].astype(o_ref.dtype)
        for gi in range(n_groups):
            z_ref[b, gi] = zs[b][gi]
        carry_ref[b] = prev_rows[b]


def _rwkv_call(p, mu, vecs, wd_pad, wa_pad, wg, chunks_per_step=2):
    bsz, seq, n_cols = p.shape
    d_r = vecs.shape[1]
    ts = CHUNK * chunks_per_step
    head = np.arange(GROUP) // HEAD_DIM
    bd = jnp.asarray(head[:, None] == head[None, :], BF16)
    ltri = jnp.asarray(np.tril(np.ones((CHUNK, CHUNK))), BF16)
    full = lambda a: pl.BlockSpec(a.shape, lambda i: (0,) * a.ndim)
    consts = (mu, vecs, wd_pad, wa_pad, wg, bd, ltri)
    return pl.pallas_call(
        _rwkv_body,
        out_shape=jax.ShapeDtypeStruct((bsz, seq, d_r), BF16),
        grid=(seq // ts,),
        in_specs=[pl.BlockSpec((bsz, ts, n_cols), lambda i: (0, i, 0))] + [full(a) for a in consts],
        out_specs=pl.BlockSpec((bsz, ts, d_r), lambda i: (0, i, 0)),
        scratch_shapes=[pltpu.VMEM((bsz, d_r // GROUP, CHUNK, GROUP), F32),
                        pltpu.VMEM((bsz, 1, n_cols), F32)],
        compiler_params=pltpu.CompilerParams(dimension_semantics=("arbitrary",),
                                             vmem_limit_bytes=VMEM_LIMIT),
        name="rwkv7_chunked",
    )(p, *consts)


def _attn_body(q_ref, kp_ref, kc_ref, vp_ref, vc_ref, bias_ref, o_ref, kn_ref, vn_ref):
    j = pl.program_id(1)
    tq = q_ref.shape[1]
    d_a = q_ref.shape[2]
    kn_ref[0:tq, :] = kp_ref[0]
    kn_ref[tq:, :] = kc_ref[0]
    vn_ref[0:tq, :] = vp_ref[0]
    vn_ref[tq:, :] = vc_ref[0]
    lane = _iota((CHUNK, LANES), 1)
    head0 = lane < HEAD_DIM
    key = _iota((CHUNK, BAND), 1)
    first_tile = j == 0
    left = tq - N_LEFT_CHUNKS * CHUNK

    def scores(ci, pr):
        sl = slice(pr * LANES, (pr + 1) * LANES)
        qp = q_ref[0, ci * CHUNK:(ci + 1) * CHUNK, sl]
        kb = kn_ref[ci * CHUNK + left:ci * CHUNK + left + BAND, sl]
        invalid = jnp.logical_and(first_tile, key < (N_LEFT_CHUNKS - ci) * CHUNK)
        es = []
        for h in range(2):
            qh = jnp.where(head0 if h == 0 else jnp.logical_not(head0), qp, jnp.zeros_like(qp))
            s = _mm(qh, kb, _NT) + bias_ref[2 * pr + h]
            if ci < N_LEFT_CHUNKS:
                s = jnp.where(invalid, -1e30, s)
            e = jnp.exp(s - jnp.max(s, axis=-1, keepdims=True))
            es.append((e.astype(BF16), jnp.sum(e, axis=-1, keepdims=True)))
        return es

    def weighted_values(ci, pr, es):
        sl = slice(pr * LANES, (pr + 1) * LANES)
        vb = vn_ref[ci * CHUNK + left:ci * CHUNK + left + BAND, sl]
        outs = [_mm(e, vb) / l for e, l in es]
        o_ref[0, ci * CHUNK:(ci + 1) * CHUNK, sl] = jnp.where(head0, outs[0], outs[1]).astype(o_ref.dtype)

    units = [(ci, pr) for ci in range(tq // CHUNK) for pr in range(d_a // LANES)]
    pending = None
    for unit in units:
        es = scores(*unit)
        if pending is not None:
            weighted_values(*pending)
        pending = (*unit, es)
    weighted_values(*pending)


def _attn_call(qkv, bias, tq=512):
    bsz, seq, three_d = qkv.shape
    d_a = three_d // 3
    assert tq >= N_LEFT_CHUNKS * CHUNK and seq % tq == 0
    cur = lambda col: pl.BlockSpec((1, tq, d_a), lambda b, i: (b, i, col))
    prev = lambda col: pl.BlockSpec((1, tq, d_a), lambda b, i: (b, jnp.maximum(i - 1, 0), col))
    full = lambda a: pl.BlockSpec(a.shape, lambda b, i: (0,) * a.ndim)
    return pl.pallas_call(
        _attn_body,
        out_shape=jax.ShapeDtypeStruct((bsz, seq, d_a), BF16),
        grid=(bsz, seq // tq),
        in_specs=[cur(0), prev(1), cur(1), prev(2), cur(2), full(bias)],
        out_specs=pl.BlockSpec((1, tq, d_a), lambda b, i: (b, i, 0)),
        scratch_shapes=[pltpu.VMEM((2 * tq, d_a), BF16), pltpu.VMEM((2 * tq, d_a), BF16)],
        compiler_params=pltpu.CompilerParams(dimension_semantics=("parallel", "parallel"),
                                             vmem_limit_bytes=VMEM_LIMIT),
        name="chunk_band_attn",
    )(qkv, qkv, qkv, qkv, qkv, bias)


def _rel_bias_table(rel_bias):
    far = N_LEFT_CHUNKS * CHUNK + CHUNK - 1 - REL_CLIP
    near = rel_bias[:, REL_CLIP - (CHUNK - 1):2 * REL_CLIP]
    ext = jnp.concatenate([jnp.repeat(rel_bias[:, 2 * REL_CLIP:], far + 1, axis=1),
                           near[:, ::-1]], axis=1)
    rows = [ext[:, CHUNK - 1 - i:CHUNK - 1 - i + BAND] for i in range(CHUNK)]
    return jnp.stack(rows, axis=1).astype(F32)


def kernel(x, c, w_ada, b_ada, norm1_g, ffn1_w1, ffn1_w3, ffn1_w2, norm2_g, w_in, mu_shift, w0, w_decay_up, a0, w_a_up, w_g_up, k_k, k_a, r_k, lnx_g, lnx_b, q_norm_g, k_norm_g, rel_bias, w_out, norm3_g, ffn2_w1, ffn2_w3, ffn2_w2):
    bsz, seq, d = x.shape
    depth = w_ada.shape[0]
    d_r = w0.shape[1]
    n_lora_w, n_lora_a = w_decay_up.shape[1], w_a_up.shape[1]
    assert n_lora_w + n_lora_a == LANES and w_g_up.shape[1] == LANES
    n_rwkv = 3 * d_r + n_lora_w + n_lora_a + w_g_up.shape[1]
    bf = lambda w: w.astype(BF16)
    h = x
    for l in range(depth):
        mod = _mod_call(c, w_ada[l], b_ada[l]).reshape(bsz, N_MOD, d)
        h = _ffn_call(h, mod, norm1_g[l], bf(ffn1_w1[l]), bf(ffn1_w3[l]), bf(ffn1_w2[l]))
        p, qkv = _proj_call(h, mod, norm2_g[l], bf(w_in[l][:, :n_rwkv]), bf(w_in[l][:, n_rwkv:]),
                            q_norm_g[l], k_norm_g[l])
        zeros_w = jnp.zeros((n_lora_a, d_r), F32)
        zeros_a = jnp.zeros((n_lora_w, d_r), F32)
        vecs = jnp.stack([w0[l], a0[l], k_k[l], k_a[l], r_k[l].reshape(-1), lnx_g[l], lnx_b[l],
                          jnp.zeros_like(w0[l])]).astype(F32)
        y_r = _rwkv_call(p, mu_shift[l].reshape(1, -1), vecs,
                         jnp.concatenate([w_decay_up[l], zeros_w], axis=0),
                         jnp.concatenate([zeros_a, w_a_up[l]], axis=0), w_g_up[l])
        y_a = _attn_call(qkv, _rel_bias_table(rel_bias[l]))
        h = _ffn_call(h, mod, norm3_g[l], bf(ffn2_w1[l]), bf(ffn2_w3[l]), bf(ffn2_w2[l]),
                      premix=(y_r, y_a, bf(w_out[l])))
    return h
```

```python
import functools

import numpy as np
import jax
import jax.numpy as jnp
from jax import lax
from jax.experimental import pallas as pl
from jax.experimental.pallas import tpu as pltpu

F32 = jnp.float32
BF16 = jnp.bfloat16

HEAD_DIM = 64
CHUNK = 64
N_LEFT_CHUNKS = 8
BAND = (N_LEFT_CHUNKS + 1) * CHUNK
REL_CLIP = 256
N_MOD = 9
NORM_EPS = 1e-6
QK_EPS = 1e-6
LNX_EPS = 64e-5
LANES = 128
GROUP = 256
VMEM_LIMIT = 56 * 1024 * 1024

_NN = (((1,), (0,)), ((), ()))
_NT = (((1,), (1,)), ((), ()))
_TN = (((0,), (0,)), ((), ()))


def _split(x, n):
    if isinstance(x, (list, tuple)):
        return list(x)
    if x.dtype == BF16:
        return [x]
    parts = []
    r = x
    for i in range(n):
        h = r.astype(BF16)
        parts.append(h)
        if i + 1 < n:
            r = r - h.astype(F32)
    return parts


def _mm(a, b, dims=_NN, pa=1, pb=1):
    ap, bp = _split(a, pa), _split(b, pb)
    order = max(len(ap), len(bp))
    acc = None
    for i, ai in enumerate(ap):
        for j, bj in enumerate(bp):
            if i + j < order:
                t = lax.dot_general(ai, bj, dims, preferred_element_type=F32)
                acc = t if acc is None else acc + t
    return acc


def _head_sums(x, bd, terms=1):
    parts = [_mm(x[:, gi * GROUP:(gi + 1) * GROUP], bd, pa=terms) for gi in range(x.shape[1] // GROUP)]
    return jnp.concatenate(parts, axis=1)


def _sigmoid(x):
    return 1.0 / (1.0 + jnp.exp(-x))


def _iota(shape, axis):
    return lax.broadcasted_iota(jnp.int32, shape, axis)


def _mod_body(c_ref, w_ref, b_ref, o_ref):
    c = c_ref[...]
    s = c * _sigmoid(c)
    o_ref[...] = _mm(s, w_ref[...], pa=2, pb=2) + b_ref[...]


def _mod_call(c, w_ada, b_ada):
    bsz, d = c.shape
    n = w_ada.shape[1]
    tn = 1024
    return pl.pallas_call(
        _mod_body,
        out_shape=jax.ShapeDtypeStruct((bsz, n), F32),
        grid=(n // tn,),
        in_specs=[pl.BlockSpec((bsz, d), lambda j: (0, 0)),
                  pl.BlockSpec((d, tn), lambda j: (0, j)),
                  pl.BlockSpec((1, tn), lambda j: (0, j))],
        out_specs=pl.BlockSpec((bsz, tn), lambda j: (0, j)),
        compiler_params=pltpu.CompilerParams(dimension_semantics=("arbitrary",),
                                             vmem_limit_bytes=VMEM_LIMIT),
        name="adaln_mod",
    )(c, w_ada, b_ada.reshape(1, n))


def _norm_mod(h, ng, sh, sc):
    y = h * lax.rsqrt(jnp.mean(h * h, axis=-1, keepdims=True) + NORM_EPS)
    return (y * ng) * (1.0 + sc) + sh


def _ffn_body(premix, ff_tile, *refs):
    if premix:
        h_ref, yr_ref, ya_ref, mod_ref, ng_ref, wo_ref, w1_ref, w3_ref, w2_ref, o_ref, act_ref = refs
    else:
        h_ref, mod_ref, ng_ref, w1_ref, w3_ref, w2_ref, o_ref, act_ref = refs
    h = h_ref[0]
    mod = mod_ref[0]
    if premix:
        dr = yr_ref.shape[-1]
        mixed = (_mm(yr_ref[0], wo_ref[0:dr, :]) + _mm(ya_ref[0], wo_ref[dr:, :]))
        h = h + mod[5:6, :] * mixed
        sh, sc, g = mod[6:7, :], mod[7:8, :], mod[8:9, :]
    else:
        sh, sc, g = mod[0:1, :], mod[1:2, :], mod[2:3, :]
    n = _norm_mod(h, ng_ref[...], sh, sc).astype(BF16)
    d_ff = w1_ref.shape[1]
    for j in range(d_ff // ff_tile):
        js = slice(j * ff_tile, (j + 1) * ff_tile)
        a = _mm(n, w1_ref[:, js])
        b = _mm(n, w3_ref[:, js])
        act_ref[:, js] = (a * _sigmoid(a) * b).astype(BF16)
    o_ref[0] = h + (0.5 * g) * _mm(act_ref[...], w2_ref[...])


def _resident(shape):
    return pl.BlockSpec(shape, lambda b, i: (0,) * len(shape), pipeline_mode=pl.Buffered(1))


def _ffn_call(h, mod, norm_g, w1, w3, w2, premix=None, tm=512, ff_tile=256):
    bsz, seq, d = h.shape
    d_ff = w1.shape[1]
    assert d_ff % ff_tile == 0 and seq % tm == 0
    tile = lambda width: pl.BlockSpec((1, tm, width), lambda b, i: (b, i, 0))
    mod_spec = pl.BlockSpec((1, N_MOD, d), lambda b, i: (b, 0, 0))
    in_specs = [tile(d)]
    args = [h]
    if premix is not None:
        y_r, y_a, w_out = premix
        in_specs += [tile(y_r.shape[-1]), tile(y_a.shape[-1])]
        args += [y_r, y_a]
    in_specs += [mod_spec, _resident((1, d))]
    args += [mod, norm_g.reshape(1, d)]
    if premix is not None:
        in_specs.append(_resident(w_out.shape))
        args.append(w_out)
    in_specs += [_resident(w1.shape), _resident(w3.shape), _resident(w2.shape)]
    args += [w1, w3, w2]
    return pl.pallas_call(
        functools.partial(_ffn_body, premix is not None, ff_tile),
        out_shape=jax.ShapeDtypeStruct((bsz, seq, d), F32),
        grid=(bsz, seq // tm),
        in_specs=in_specs,
        out_specs=tile(d),
        scratch_shapes=[pltpu.VMEM((tm, d_ff), BF16)],
        compiler_params=pltpu.CompilerParams(dimension_semantics=("parallel", "parallel"),
                                             vmem_limit_bytes=VMEM_LIMIT),
        name="ffn_premix" if premix is not None else "ffn",
    )(*args)


def _head_rms(x, gain, bd, eps):
    ms = _head_sums(x * x, bd) * (1.0 / HEAD_DIM)
    return x * lax.rsqrt(ms + eps) * gain


def _token_shift(p, prev_row, mu, roll):
    first = _iota(p.shape, 0) == 0
    p_prev = jnp.where(first, prev_row, roll(p, 1, 0))
    return p + (p_prev - p) * mu


def _proj_body(h_ref, mod_ref, ng_ref, wr_ref, wa_ref, qg_ref, kg_ref, bd_ref, mu_ref, p_ref, qkv_ref,
               carry_ref):
    @pl.when(pl.program_id(1) == 0)
    def _():
        carry_ref[...] = jnp.zeros_like(carry_ref)

    mod = mod_ref[0]
    n = _norm_mod(h_ref[0], ng_ref[...], mod[3:4, :], mod[4:5, :]).astype(BF16)
    p = _mm(n, wr_ref[...])
    p_ref[0] = _token_shift(p, carry_ref[...], mu_ref[...], pltpu.roll)
    carry_ref[...] = p[p.shape[0] - 1:, :]
    d_a = qg_ref.shape[1]
    bd = bd_ref[...]
    q = _head_rms(_mm(n, wa_ref[:, 0:d_a]), qg_ref[...], bd, QK_EPS) * (HEAD_DIM ** -0.5)
    qkv_ref[0, :, 0:d_a] = q.astype(BF16)
    k = _head_rms(_mm(n, wa_ref[:, d_a:2 * d_a]), kg_ref[...], bd, QK_EPS)
    qkv_ref[0, :, d_a:2 * d_a] = k.astype(BF16)
    qkv_ref[0, :, 2 * d_a:] = _mm(n, wa_ref[:, 2 * d_a:]).astype(BF16)


def _proj_call(h, mod, norm_g, w_r, w_a, q_norm_g, k_norm_g, mu, tm=512):
    bsz, seq, d = h.shape
    d_a = w_a.shape[1] // 3
    n_heads = d_a // HEAD_DIM
    head = np.arange(GROUP) // HEAD_DIM
    bd = jnp.asarray(head[:, None] == head[None, :], BF16)
    qg = jnp.tile(q_norm_g.astype(F32), n_heads).reshape(1, d_a)
    kg = jnp.tile(k_norm_g.astype(F32), n_heads).reshape(1, d_a)
    tile = lambda width: pl.BlockSpec((1, tm, width), lambda b, i: (b, i, 0))
    return pl.pallas_call(
        _proj_body,
        out_shape=(jax.ShapeDtypeStruct((bsz, seq, w_r.shape[1]), F32),
                   jax.ShapeDtypeStruct((bsz, seq, w_a.shape[1]), BF16)),
        grid=(bsz, seq // tm),
        in_specs=[tile(d), pl.BlockSpec((1, N_MOD, d), lambda b, i: (b, 0, 0)), _resident((1, d)),
                  _resident(w_r.shape), _resident(w_a.shape), _resident((1, d_a)),
                  _resident((1, d_a)), _resident(bd.shape), _resident(mu.shape)],
        out_specs=(tile(w_r.shape[1]), tile(w_a.shape[1])),
        scratch_shapes=[pltpu.VMEM((1, w_r.shape[1]), F32)],
        compiler_params=pltpu.CompilerParams(dimension_semantics=("arbitrary", "arbitrary"),
                                             vmem_limit_bytes=VMEM_LIMIT),
        name="in_proj",
    )(h, mod, norm_g.reshape(1, d), w_r, w_a, qg, kg, bd, mu)


SOLVE_TERMS = (1, 1)
STATE_TERMS = (2, 1)
DECAY_TERMS = (2, 2)
RATE_TERMS = (1, 1)
GATE_TERMS = (2, 2)
SUM_TERMS = (1, 1, 1, 1)
CUMSUM_TERMS = 2


def _block_diag(x):
    if isinstance(x, (list, tuple)):
        return [_block_diag(t) for t in x]
    n = x.shape[1] // HEAD_DIM
    tiled = jnp.concatenate([x] * n, axis=0)
    keep = (_iota(tiled.shape, 0) // HEAD_DIM) == (_iota(tiled.shape, 1) // HEAD_DIM)
    return jnp.where(keep, tiled, jnp.zeros_like(tiled))


def _diag_blocks(full):
    n = full.shape[1] // HEAD_DIM
    lane_head = _iota((HEAD_DIM, full.shape[1]), 1) // HEAD_DIM
    out = None
    for h in range(n):
        part = jnp.where(lane_head == h, full[h * HEAD_DIM:(h + 1) * HEAD_DIM, :], 0.0)
        out = part if out is None else out + part
    return out


def _mm_rows(lhs_list, rhs, pa=1, pb=1):
    rows = [l.shape[0] for l in lhs_list]
    lparts = [_split(l, pa) for l in lhs_list]
    rparts = _split(rhs, pb)
    order = max(pa, len(rparts))
    outs = [None] * len(lhs_list)
    for j, rj in enumerate(rparts):
        n_i = min(pa, order - j)
        stack = jnp.concatenate([lp[i] for i in range(n_i) for lp in lparts], axis=0)
        res = lax.dot_general(stack, rj, _NN, preferred_element_type=F32)
        off = 0
        for i in range(n_i):
            for k, nrow in enumerate(rows):
                piece = res[off:off + nrow]
                outs[k] = piece if outs[k] is None else outs[k] + piece
                off += nrow
    return outs


def _unit_lower_inverse(n_mats):
    shape = n_mats[0].shape
    r, c = _iota(shape, 0), _iota(shape, 1) % HEAD_DIM
    same_block = (r // 16) == (c // 16)
    eye = jnp.where(r == c, 1.0, 0.0).astype(F32)
    bd = lambda x: _block_diag(_split(x, SOLVE_TERMS[1]))
    mm = lambda ls, w: _mm_rows(ls, w, pa=SOLVE_TERMS[0], pb=SOLVE_TERMS[1])
    nd = [jnp.where(same_block, n, 0.0) for n in n_mats]
    no = [n - d for n, d in zip(n_mats, nd)]
    acc = [eye + d for d in nd]
    pw = [mm([d], bd(d))[0] for d in nd]
    for last in (False, False, True):
        res = [mm([t] if last else [w, t], bd(w)) for w, t in zip(pw, acc)]
        acc = [t + r_[-1] for t, r_ in zip(acc, res)]
        if not last:
            pw = [r_[0] for r_ in res]
    p = [mm([t], bd(o))[0] for t, o in zip(acc, no)]
    p2 = [mm([q], bd(q))[0] for q in p]
    acc = [t + mm([q], bd(t))[0] for q, t in zip(p, acc)]
    return [t + mm([q], bd(t))[0] for q, t in zip(p2, acc)]


def _rwkv_local(items):
    size = items[0]["at"].shape[0]
    row = _iota((size, GROUP), 0)
    col = _iota((size, GROUP), 1) % HEAD_DIM
    strict, incl = col < row, col <= row
    bd = lambda x: _block_diag(x.astype(BF16))

    m = [_mm(jnp.concatenate([it["at"], it["rt"]], axis=0),
             jnp.concatenate([bd(it["bc"]), bd(it["kc"])], axis=0), _NT) for it in items]
    n_mat = [jnp.where(strict, mi[:size, :GROUP], 0.0) for mi in m]
    mak = [jnp.where(strict, mi[:size, GROUP:], 0.0) for mi in m]
    mrb = [jnp.where(incl, mi[size:, :GROUP], 0.0) for mi in m]
    mrk = [jnp.where(incl, mi[size:, GROUP:], 0.0) for mi in m]
    mv = [_mm_rows([a, b], bd(it["v"])) for a, b, it in zip(mak, mrk, items)]
    t_inv = _unit_lower_inverse(n_mat)
    wu = [_mm(t, jnp.concatenate([bd(it["at"]), bd(x[0])], axis=1)) for t, it, x in zip(t_inv, items, mv)]
    qy = [_mm(b, jnp.concatenate([bd(x[:, :GROUP]), bd(x[:, GROUP:])], axis=1)) for b, x in zip(mrb, wu)]
    out = []
    for i, it in enumerate(items):
        w, u0 = wu[i][:, :GROUP], wu[i][:, GROUP:]
        g_mat = _diag_blocks(_mm(it["bh"], w, _TN)) + jnp.where(row == col, it["g_last"], 0.0)
        d_mat = _diag_blocks(_mm(jnp.concatenate([it["bh"], it["kh"]], axis=0),
                                 jnp.concatenate([u0, it["v"]], axis=0), _TN))
        out.append((it["rt"] + qy[i][:, :GROUP], qy[i][:, GROUP:] + mv[i][1], g_mat, d_mat))
    return out


def _rwkv_prep(xs, prm):
    vecs, wd_pad, wa_pad, wg, bd, ltri = prm
    size = xs[0][0].shape[0]
    d_r = vecs.shape[1]
    w0, a0, k_k, k_a, r_k = [vecs[i:i + 1, :] for i in range(5)]
    flat = [x for chunks in xs for x in chunks]
    x = jnp.concatenate(flat, axis=0) if len(flat) > 1 else flat[0]
    r, k, v = x[:, 0:d_r], x[:, d_r:2 * d_r], x[:, 2 * d_r:3 * d_r]
    lora = x[:, 3 * d_r:3 * d_r + LANES]
    gd = x[:, 3 * d_r + LANES:]
    lw = -float(np.exp(-0.5)) * _sigmoid(w0 + _mm(jnp.tanh(lora), wd_pad, pa=DECAY_TERMS[0]))
    a = _sigmoid(a0 + _mm(lora, wa_pad, pa=RATE_TERMS[0]))
    g = _mm(_sigmoid(gd), wg, pa=GATE_TERMS[0])
    kk = k * k_k
    kk = kk * lax.rsqrt(jnp.maximum(_head_sums(kk * kk, bd, SUM_TERMS[0]), 1e-24))
    k2 = k * (1.0 + (a - 1.0) * k_a)
    bv = kk * a
    bonus = _head_sums(r * k2 * r_k, bd, SUM_TERMS[1]) * v
    out = []
    for ci in range(x.shape[0] // size):
        rows = slice(ci * size, (ci + 1) * size)
        lw_c = lw[rows]
        c = _mm(ltri, lw_c, pb=CUMSUM_TERMS)
        g_last = jnp.exp(c[size - 1:size, :])
        e_neg = jnp.exp(-c)
        e_tail = g_last * e_neg
        full = dict(at=-kk[rows] * jnp.exp(c - lw_c), rt=r[rows] * jnp.exp(c), bc=bv[rows] * e_neg,
                    kc=k2[rows] * e_neg, bh=bv[rows] * e_tail, kh=k2[rows] * e_tail, v=v[rows],
                    g_last=g_last)
        items = [{name: val[:, gi * GROUP:(gi + 1) * GROUP] for name, val in full.items()}
                 for gi in range(d_r // GROUP)]
        out.append((items, bonus[rows], g[rows]))
    return out


def _rwkv_finish(ys, bonus, g, prm):
    vecs, bd = prm[0], prm[4]
    lnx_g, lnx_b = vecs[5:6, :], vecs[6:7, :]
    inv = 1.0 / HEAD_DIM
    mean = _head_sums(ys, bd, SUM_TERMS[2]) * inv
    dlt = ys - mean
    var = _head_sums(dlt * dlt, bd, SUM_TERMS[3]) * inv
    y = dlt * lax.rsqrt(var + LNX_EPS) * lnx_g + lnx_b
    return (y + bonus) * g


def _rwkv_chunks(xs, zs, prm):
    n_seq, n_chunks, n_groups = len(xs), len(xs[0]), len(zs[0])
    size = xs[0][0].shape[0]
    preps = _rwkv_prep(xs, prm)
    local = _rwkv_local([it for items, _, _ in preps for it in items])
    zs = [list(z) for z in zs]
    outs = [[None] * n_chunks for _ in range(n_seq)]
    for ci in range(n_chunks):
        ys = [[None] * n_groups for _ in range(n_seq)]
        for b in range(n_seq):
            for gi in range(n_groups):
                q, y0, g_mat, d_mat = local[(b * n_chunks + ci) * n_groups + gi]
                z_bd = _block_diag(_split(zs[b][gi], STATE_TERMS[1]))
                qz, gz = _mm_rows([q, g_mat], z_bd, pa=STATE_TERMS[0], pb=STATE_TERMS[1])
                ys[b][gi] = qz + y0
                zs[b][gi] = gz + d_mat
        y_all = jnp.concatenate([jnp.concatenate(ys[b], axis=1) for b in range(n_seq)], axis=0)
        bonus = jnp.concatenate([preps[b * n_chunks + ci][1] for b in range(n_seq)], axis=0)
        gate = jnp.concatenate([preps[b * n_chunks + ci][2] for b in range(n_seq)], axis=0)
        o_all = _rwkv_finish(y_all, bonus, gate, prm)
        for b in range(n_seq):
            outs[b][ci] = o_all[b * size:(b + 1) * size]
    return outs, zs


def _rwkv_body(x_ref, vec_ref, wd_ref, wa_ref, wg_ref, bd_ref, lt_ref, o_ref, z_ref):
    @pl.when(pl.program_id(0) == 0)
    def _():
        z_ref[...] = jnp.zeros_like(z_ref)

    terms = lambda ref: [ref[i] for i in range(ref.shape[0])]
    prm = (vec_ref[...], terms(wd_ref), terms(wa_ref), terms(wg_ref), bd_ref[...], lt_ref[...])
    n_seq, n_groups = z_ref.shape[0], z_ref.shape[1]
    n_chunks = x_ref.shape[1] // CHUNK
    xs = [[x_ref[b, ci * CHUNK:(ci + 1) * CHUNK, :] for ci in range(n_chunks)] for b in range(n_seq)]
    zs = [[z_ref[b, gi] for gi in range(n_groups)] for b in range(n_seq)]
    outs, zs = _rwkv_chunks(xs, zs, prm)
    for b in range(n_seq):
        for ci in range(n_chunks):
            o_ref[b, ci * CHUNK:(ci + 1) * CHUNK, :] = outs[b][ci].astype(o_ref.dtype)
        for gi in range(n_groups):
            z_ref[b, gi] = zs[b][gi]


def _bf16_terms(w, n):
    return jnp.stack(_split(w.astype(F32), n))


def _rwkv_call(x, vecs, wd_pad, wa_pad, wg, chunks_per_step=2):
    bsz, seq, n_cols = x.shape
    d_r = vecs.shape[1]
    ts = CHUNK * chunks_per_step
    head = np.arange(GROUP) // HEAD_DIM
    bd = jnp.asarray(head[:, None] == head[None, :], BF16)
    ltri = jnp.asarray(np.tril(np.ones((CHUNK, CHUNK))), BF16)
    full = lambda a: pl.BlockSpec(a.shape, lambda i: (0,) * a.ndim)
    consts = (vecs, _bf16_terms(wd_pad, DECAY_TERMS[1]), _bf16_terms(wa_pad, RATE_TERMS[1]),
              _bf16_terms(wg, GATE_TERMS[1]), bd, ltri)
    return pl.pallas_call(
        _rwkv_body,
        out_shape=jax.ShapeDtypeStruct((bsz, seq, d_r), BF16),
        grid=(seq // ts,),
        in_specs=[pl.BlockSpec((bsz, ts, n_cols), lambda i: (0, i, 0))] + [full(a) for a in consts],
        out_specs=pl.BlockSpec((bsz, ts, d_r), lambda i: (0, i, 0)),
        scratch_shapes=[pltpu.VMEM((bsz, d_r // GROUP, CHUNK, GROUP), F32)],
        compiler_params=pltpu.CompilerParams(dimension_semantics=("arbitrary",),
                                             vmem_limit_bytes=VMEM_LIMIT),
        name="rwkv7_chunked",
    )(x, *consts)


def _attn_body(q_ref, kp_ref, kc_ref, vp_ref, vc_ref, bias_ref, o_ref, kn_ref, vn_ref):
    j = pl.program_id(1)
    tq = q_ref.shape[1]
    d_a = q_ref.shape[2]
    kn_ref[0:tq, :] = kp_ref[0]
    kn_ref[tq:, :] = kc_ref[0]
    vn_ref[0:tq, :] = vp_ref[0]
    vn_ref[tq:, :] = vc_ref[0]
    lane = _iota((CHUNK, LANES), 1)
    head0 = lane < HEAD_DIM
    key = _iota((CHUNK, BAND), 1)
    first_tile = j == 0
    left = tq - N_LEFT_CHUNKS * CHUNK

    def scores(ci, pr):
        sl = slice(pr * LANES, (pr + 1) * LANES)
        qp = q_ref[0, ci * CHUNK:(ci + 1) * CHUNK, sl]
        kb = kn_ref[ci * CHUNK + left:ci * CHUNK + left + BAND, sl]
        invalid = jnp.logical_and(first_tile, key < (N_LEFT_CHUNKS - ci) * CHUNK)
        zero = jnp.zeros_like(qp)
        q2 = jnp.concatenate([jnp.where(head0, qp, zero), jnp.where(head0, zero, qp)], axis=0)
        s2 = _mm(q2, kb, _NT)
        es = []
        for h in range(2):
            s = s2[h * CHUNK:(h + 1) * CHUNK] + bias_ref[2 * pr + h]
            if ci < N_LEFT_CHUNKS:
                s = jnp.where(invalid, -1e30, s)
            e = jnp.exp(s - jnp.max(s, axis=-1, keepdims=True))
            es.append((e.astype(BF16), jnp.sum(e, axis=-1, keepdims=True)))
        return es

    def weighted_values(ci, pr, es):
        sl = slice(pr * LANES, (pr + 1) * LANES)
        vb = vn_ref[ci * CHUNK + left:ci * CHUNK + left + BAND, sl]
        o2 = _mm(jnp.concatenate([e for e, _ in es], axis=0), vb)
        outs = [o2[h * CHUNK:(h + 1) * CHUNK] / es[h][1] for h in range(2)]
        o_ref[0, ci * CHUNK:(ci + 1) * CHUNK, sl] = jnp.where(head0, outs[0], outs[1]).astype(o_ref.dtype)

    units = [(ci, pr) for ci in range(tq // CHUNK) for pr in range(d_a // LANES)]
    pending = None
    for unit in units:
        es = scores(*unit)
        if pending is not None:
            weighted_values(*pending)
        pending = (*unit, es)
    weighted_values(*pending)


def _attn_call(qkv, bias, tq=512):
    bsz, seq, three_d = qkv.shape
    d_a = three_d // 3
    assert tq >= N_LEFT_CHUNKS * CHUNK and seq % tq == 0
    cur = lambda col: pl.BlockSpec((1, tq, d_a), lambda b, i: (b, i, col))
    prev = lambda col: pl.BlockSpec((1, tq, d_a), lambda b, i: (b, jnp.maximum(i - 1, 0), col))
    full = lambda a: pl.BlockSpec(a.shape, lambda b, i: (0,) * a.ndim)
    return pl.pallas_call(
        _attn_body,
        out_shape=jax.ShapeDtypeStruct((bsz, seq, d_a), BF16),
        grid=(bsz, seq // tq),
        in_specs=[cur(0), prev(1), cur(1), prev(2), cur(2), full(bias)],
        out_specs=pl.BlockSpec((1, tq, d_a), lambda b, i: (b, i, 0)),
        scratch_shapes=[pltpu.VMEM((2 * tq, d_a), BF16), pltpu.VMEM((2 * tq, d_a), BF16)],
        compiler_params=pltpu.CompilerParams(dimension_semantics=("parallel", "parallel"),
                                             vmem_limit_bytes=VMEM_LIMIT),
        name="chunk_band_attn",
    )(qkv, qkv, qkv, qkv, qkv, bias)


def _rel_bias_table(rel_bias):
    far = N_LEFT_CHUNKS * CHUNK + CHUNK - 1 - REL_CLIP
    near = rel_bias[:, REL_CLIP - (CHUNK - 1):2 * REL_CLIP]
    ext = jnp.concatenate([jnp.repeat(rel_bias[:, 2 * REL_CLIP:], far + 1, axis=1),
                           near[:, ::-1]], axis=1)
    rows = [ext[:, CHUNK - 1 - i:CHUNK - 1 - i + BAND] for i in range(CHUNK)]
    return jnp.stack(rows, axis=1).astype(F32)


def kernel(x, c, w_ada, b_ada, norm1_g, ffn1_w1, ffn1_w3, ffn1_w2, norm2_g, w_in, mu_shift, w0, w_decay_up, a0, w_a_up, w_g_up, k_k, k_a, r_k, lnx_g, lnx_b, q_norm_g, k_norm_g, rel_bias, w_out, norm3_g, ffn2_w1, ffn2_w3, ffn2_w2):
    bsz, seq, d = x.shape
    depth = w_ada.shape[0]
    d_r = w0.shape[1]
    n_lora_w, n_lora_a = w_decay_up.shape[1], w_a_up.shape[1]
    assert n_lora_w + n_lora_a == LANES and w_g_up.shape[1] == LANES
    n_rwkv = 3 * d_r + n_lora_w + n_lora_a + w_g_up.shape[1]
    bf = lambda w: w.astype(BF16)
    h = x
    for l in range(depth):
        mod = _mod_call(c, w_ada[l], b_ada[l]).reshape(bsz, N_MOD, d)
        h = _ffn_call(h, mod, norm1_g[l], bf(ffn1_w1[l]), bf(ffn1_w3[l]), bf(ffn1_w2[l]))
        p, qkv = _proj_call(h, mod, norm2_g[l], bf(w_in[l][:, :n_rwkv]), bf(w_in[l][:, n_rwkv:]),
                            q_norm_g[l], k_norm_g[l], mu_shift[l].reshape(1, -1))
        zeros_w = jnp.zeros((n_lora_a, d_r), F32)
        zeros_a = jnp.zeros((n_lora_w, d_r), F32)
        vecs = jnp.stack([w0[l], a0[l], k_k[l], k_a[l], r_k[l].reshape(-1), lnx_g[l], lnx_b[l],
                          jnp.zeros_like(w0[l])]).astype(F32)
        y_r = _rwkv_call(p, vecs,
                         jnp.concatenate([w_decay_up[l], zeros_w], axis=0),
                         jnp.concatenate([zeros_a, w_a_up[l]], axis=0), w_g_up[l])
        y_a = _attn_call(qkv, _rel_bias_table(rel_bias[l]))
        h = _ffn_call(h, mod, norm3_g[l], bf(ffn2_w1[l]), bf(ffn2_w3[l]), bf(ffn2_w2[l]),
                      premix=(y_r, y_a, bf(w_out[l])))
    return h
```

```python
import functools

import numpy as np
import jax
import jax.numpy as jnp
from jax import lax
from jax.experimental import pallas as pl
from jax.experimental.pallas import tpu as pltpu

F32 = jnp.float32
BF16 = jnp.bfloat16

HEAD_DIM = 64
CHUNK = 64
N_LEFT_CHUNKS = 8
BAND = (N_LEFT_CHUNKS + 1) * CHUNK
REL_CLIP = 256
N_MOD = 9
NORM_EPS = 1e-6
QK_EPS = 1e-6
LNX_EPS = 64e-5
LANES = 128
GROUP = 256
VMEM_LIMIT = 56 * 1024 * 1024
LOG2_E = float(np.log2(np.e))

_NN = (((1,), (0,)), ((), ()))
_NT = (((1,), (1,)), ((), ()))
_TN = (((0,), (0,)), ((), ()))


def _split(x, n):
    if isinstance(x, (list, tuple)):
        return list(x)
    if x.dtype == BF16:
        return [x]
    parts = []
    r = x
    for i in range(n):
        h = r.astype(BF16)
        parts.append(h)
        if i + 1 < n:
            r = r - h.astype(F32)
    return parts


def _mm(a, b, dims=_NN, pa=1, pb=1):
    ap, bp = _split(a, pa), _split(b, pb)
    order = max(len(ap), len(bp))
    acc = None
    for i, ai in enumerate(ap):
        for j, bj in enumerate(bp):
            if i + j < order:
                t = lax.dot_general(ai, bj, dims, preferred_element_type=F32)
                acc = t if acc is None else acc + t
    return acc


def _head_sums(x, bd, terms=1):
    parts = [_mm(x[:, gi * GROUP:(gi + 1) * GROUP], bd, pa=terms) for gi in range(x.shape[1] // GROUP)]
    return jnp.concatenate(parts, axis=1)


def _sigmoid(x):
    return 1.0 / (1.0 + jnp.exp(-x))


def _iota(shape, axis):
    return lax.broadcasted_iota(jnp.int32, shape, axis)


def _mod_body(c_ref, w_ref, b_ref, o_ref):
    c = c_ref[...]
    s = c * _sigmoid(c)
    o_ref[...] = _mm(s, w_ref[...], pa=2, pb=2) + b_ref[...]


def _mod_call(c, w_ada, b_ada):
    bsz, d = c.shape
    n = w_ada.shape[1]
    tn = 1024
    return pl.pallas_call(
        _mod_body,
        out_shape=jax.ShapeDtypeStruct((bsz, n), F32),
        grid=(n // tn,),
        in_specs=[pl.BlockSpec((bsz, d), lambda j: (0, 0)),
                  pl.BlockSpec((d, tn), lambda j: (0, j)),
                  pl.BlockSpec((1, tn), lambda j: (0, j))],
        out_specs=pl.BlockSpec((bsz, tn), lambda j: (0, j)),
        compiler_params=pltpu.CompilerParams(dimension_semantics=("arbitrary",),
                                             vmem_limit_bytes=VMEM_LIMIT),
        name="adaln_mod",
    )(c, w_ada, b_ada.reshape(1, n))


def _norm_mod(h, ng, sh, sc):
    y = h * lax.rsqrt(jnp.mean(h * h, axis=-1, keepdims=True) + NORM_EPS)
    return (y * ng) * (1.0 + sc) + sh


def _ffn_body(premix, ff_tile, *refs):
    if premix:
        h_ref, yr_ref, ya_ref, mod_ref, ng_ref, wo_ref, w1_ref, w3_ref, w2_ref, o_ref, act_ref = refs
    else:
        h_ref, mod_ref, ng_ref, w1_ref, w3_ref, w2_ref, o_ref, act_ref = refs
    h = h_ref[0]
    mod = mod_ref[0]
    if premix:
        dr = yr_ref.shape[-1]
        mixed = (_mm(yr_ref[0], wo_ref[0:dr, :]) + _mm(ya_ref[0], wo_ref[dr:, :]))
        h = h + mod[5:6, :] * mixed
        sh, sc, g = mod[6:7, :], mod[7:8, :], mod[8:9, :]
    else:
        sh, sc, g = mod[0:1, :], mod[1:2, :], mod[2:3, :]
    n = _norm_mod(h, ng_ref[...], sh, sc).astype(BF16)
    d_ff = w1_ref.shape[1]
    for j in range(d_ff // ff_tile):
        js = slice(j * ff_tile, (j + 1) * ff_tile)
        a = _mm(n, w1_ref[:, js])
        b = _mm(n, w3_ref[:, js])
        act_ref[:, js] = (a * _sigmoid(a) * b).astype(BF16)
    o_ref[0] = h + (0.5 * g) * _mm(act_ref[...], w2_ref[...])


def _resident(shape):
    return pl.BlockSpec(shape, lambda b, i: (0,) * len(shape), pipeline_mode=pl.Buffered(1))


def _ffn_call(h, mod, norm_g, w1, w3, w2, premix=None, tm=512, ff_tile=256):
    bsz, seq, d = h.shape
    d_ff = w1.shape[1]
    assert d_ff % ff_tile == 0 and seq % tm == 0
    tile = lambda width: pl.BlockSpec((1, tm, width), lambda b, i: (b, i, 0))
    mod_spec = pl.BlockSpec((1, N_MOD, d), lambda b, i: (b, 0, 0))
    in_specs = [tile(d)]
    args = [h]
    if premix is not None:
        y_r, y_a, w_out = premix
        in_specs += [tile(y_r.shape[-1]), tile(y_a.shape[-1])]
        args += [y_r, y_a]
    in_specs += [mod_spec, _resident((1, d))]
    args += [mod, norm_g.reshape(1, d)]
    if premix is not None:
        in_specs.append(_resident(w_out.shape))
        args.append(w_out)
    in_specs += [_resident(w1.shape), _resident(w3.shape), _resident(w2.shape)]
    args += [w1, w3, w2]
    return pl.pallas_call(
        functools.partial(_ffn_body, premix is not None, ff_tile),
        out_shape=jax.ShapeDtypeStruct((bsz, seq, d), F32),
        grid=(bsz, seq // tm),
        in_specs=in_specs,
        out_specs=tile(d),
        scratch_shapes=[pltpu.VMEM((tm, d_ff), BF16)],
        compiler_params=pltpu.CompilerParams(dimension_semantics=("parallel", "parallel"),
                                             vmem_limit_bytes=VMEM_LIMIT),
        name="ffn_premix" if premix is not None else "ffn",
    )(*args)


def _head_rms(x, gain, bd, eps):
    ms = _head_sums(x * x, bd) * (1.0 / HEAD_DIM)
    return x * lax.rsqrt(ms + eps) * gain


def _token_shift(p, prev_row, mu, roll):
    first = _iota(p.shape, 0) == 0
    p_prev = jnp.where(first, prev_row, roll(p, 1, 0))
    return p + (p_prev - p) * mu


def _proj_body(h_ref, mod_ref, ng_ref, w_ref, qg_ref, kg_ref, bd_ref, mu_ref, p_ref, qkv_ref,
               carry_ref):
    @pl.when(pl.program_id(1) == 0)
    def _():
        carry_ref[...] = jnp.zeros_like(carry_ref)

    mod = mod_ref[0]
    n = _norm_mod(h_ref[0], ng_ref[...], mod[3:4, :], mod[4:5, :]).astype(BF16)
    n_r = p_ref.shape[2]
    d_a = qg_ref.shape[1]
    p = _mm(n, w_ref[:, 0:n_r])
    p_ref[0] = _token_shift(p, carry_ref[...], mu_ref[...], pltpu.roll)
    carry_ref[...] = p[p.shape[0] - 1:, :]
    bd = bd_ref[...]
    q = _head_rms(_mm(n, w_ref[:, n_r:n_r + d_a]), qg_ref[...], bd, QK_EPS) * (HEAD_DIM ** -0.5 * LOG2_E)
    qkv_ref[0, :, 0:d_a] = q.astype(BF16)
    k = _head_rms(_mm(n, w_ref[:, n_r + d_a:n_r + 2 * d_a]), kg_ref[...], bd, QK_EPS)
    qkv_ref[0, :, d_a:2 * d_a] = k.astype(BF16)
    qkv_ref[0, :, 2 * d_a:] = _mm(n, w_ref[:, n_r + 2 * d_a:]).astype(BF16)


def _proj_call(h, mod, norm_g, w_in, n_r, q_norm_g, k_norm_g, mu, tm=512):
    bsz, seq, d = h.shape
    n_a = w_in.shape[1] - n_r
    d_a = n_a // 3
    n_heads = d_a // HEAD_DIM
    head = np.arange(GROUP) // HEAD_DIM
    bd = jnp.asarray(head[:, None] == head[None, :], BF16)
    qg = jnp.tile(q_norm_g.astype(F32), n_heads).reshape(1, d_a)
    kg = jnp.tile(k_norm_g.astype(F32), n_heads).reshape(1, d_a)
    tile = lambda width: pl.BlockSpec((1, tm, width), lambda b, i: (b, i, 0))
    return pl.pallas_call(
        _proj_body,
        out_shape=(jax.ShapeDtypeStruct((bsz, seq, n_r), F32),
                   jax.ShapeDtypeStruct((bsz, seq, n_a), BF16)),
        grid=(bsz, seq // tm),
        in_specs=[tile(d), pl.BlockSpec((1, N_MOD, d), lambda b, i: (b, 0, 0)), _resident((1, d)),
                  _resident(w_in.shape), _resident((1, d_a)), _resident((1, d_a)),
                  _resident(bd.shape), _resident(mu.shape)],
        out_specs=(tile(n_r), tile(n_a)),
        scratch_shapes=[pltpu.VMEM((1, n_r), F32)],
        compiler_params=pltpu.CompilerParams(dimension_semantics=("arbitrary", "arbitrary"),
                                             vmem_limit_bytes=VMEM_LIMIT),
        name="in_proj",
    )(h, mod, norm_g.reshape(1, d), w_in, qg, kg, bd, mu)


SOLVE_TERMS = (1, 1)
STATE_TERMS = (2, 1)
DECAY_TERMS = (2, 2)
RATE_TERMS = (1, 1)
GATE_TERMS = (2, 2)
SUM_TERMS = (1, 1, 1, 1)
CUMSUM_TERMS = 2


def _block_diag(x):
    if isinstance(x, (list, tuple)):
        return [_block_diag(t) for t in x]
    n = x.shape[1] // HEAD_DIM
    tiled = jnp.concatenate([x] * n, axis=0)
    keep = (_iota(tiled.shape, 0) // HEAD_DIM) == (_iota(tiled.shape, 1) // HEAD_DIM)
    return jnp.where(keep, tiled, jnp.zeros_like(tiled))


def _diag_blocks(full):
    n = full.shape[1] // HEAD_DIM
    lane_head = _iota((HEAD_DIM, full.shape[1]), 1) // HEAD_DIM
    out = None
    for h in range(n):
        part = jnp.where(lane_head == h, full[h * HEAD_DIM:(h + 1) * HEAD_DIM, :], 0.0)
        out = part if out is None else out + part
    return out


def _mm_rows(lhs_list, rhs, pa=1, pb=1):
    rows = [l.shape[0] for l in lhs_list]
    lparts = [_split(l, pa) for l in lhs_list]
    rparts = _split(rhs, pb)
    order = max(pa, len(rparts))
    outs = [None] * len(lhs_list)
    for j, rj in enumerate(rparts):
        n_i = min(pa, order - j)
        stack = jnp.concatenate([lp[i] for i in range(n_i) for lp in lparts], axis=0)
        res = lax.dot_general(stack, rj, _NN, preferred_element_type=F32)
        off = 0
        for i in range(n_i):
            for k, nrow in enumerate(rows):
                piece = res[off:off + nrow]
                outs[k] = piece if outs[k] is None else outs[k] + piece
                off += nrow
    return outs


def _unit_lower_inverse(n_mats):
    shape = n_mats[0].shape
    r, c = _iota(shape, 0), _iota(shape, 1) % HEAD_DIM
    same_block = (r // 16) == (c // 16)
    eye = jnp.where(r == c, 1.0, 0.0).astype(F32)
    bd = lambda x: _block_diag(_split(x, SOLVE_TERMS[1]))
    mm = lambda ls, w: _mm_rows(ls, w, pa=SOLVE_TERMS[0], pb=SOLVE_TERMS[1])
    nd = [jnp.where(same_block, n, 0.0) for n in n_mats]
    no = [n - d for n, d in zip(n_mats, nd)]
    acc = [eye + d for d in nd]
    pw = [mm([d], bd(d))[0] for d in nd]
    for last in (False, False, True):
        res = [mm([t] if last else [w, t], bd(w)) for w, t in zip(pw, acc)]
        acc = [t + r_[-1] for t, r_ in zip(acc, res)]
        if not last:
            pw = [r_[0] for r_ in res]
    p = [mm([t], bd(o))[0] for t, o in zip(acc, no)]
    p2 = [mm([q], bd(q))[0] for q in p]
    acc = [t + mm([q], bd(t))[0] for q, t in zip(p, acc)]
    return [t + mm([q], bd(t))[0] for q, t in zip(p2, acc)]


def _rwkv_local(items):
    size = items[0]["at"].shape[0]
    row = _iota((size, GROUP), 0)
    col = _iota((size, GROUP), 1) % HEAD_DIM
    strict, incl = col < row, col <= row
    bd = lambda x: _block_diag(x.astype(BF16))

    m = [_mm(jnp.concatenate([it["at"], it["rt"]], axis=0),
             jnp.concatenate([bd(it["bc"]), bd(it["kc"])], axis=0), _NT) for it in items]
    n_mat = [jnp.where(strict, mi[:size, :GROUP], 0.0) for mi in m]
    mak = [jnp.where(strict, mi[:size, GROUP:], 0.0) for mi in m]
    mrb = [jnp.where(incl, mi[size:, :GROUP], 0.0) for mi in m]
    mrk = [jnp.where(incl, mi[size:, GROUP:], 0.0) for mi in m]
    mv = [_mm_rows([a, b], bd(it["v"])) for a, b, it in zip(mak, mrk, items)]
    t_inv = _unit_lower_inverse(n_mat)
    wu = [_mm(t, jnp.concatenate([bd(it["at"]), bd(x[0])], axis=1)) for t, it, x in zip(t_inv, items, mv)]
    qy = [_mm(b, jnp.concatenate([bd(x[:, :GROUP]), bd(x[:, GROUP:])], axis=1)) for b, x in zip(mrb, wu)]
    out = []
    for i, it in enumerate(items):
        w, u0 = wu[i][:, :GROUP], wu[i][:, GROUP:]
        g_mat = _diag_blocks(_mm(it["bh"], w, _TN)) + jnp.where(row == col, it["g_last"], 0.0)
        d_mat = _diag_blocks(_mm(jnp.concatenate([it["bh"], it["kh"]], axis=0),
                                 jnp.concatenate([u0, it["v"]], axis=0), _TN))
        out.append((it["rt"] + qy[i][:, :GROUP], qy[i][:, GROUP:] + mv[i][1], g_mat, d_mat))
    return out


def _rwkv_prep(xs, prm):
    vecs, wd_pad, wa_pad, wg, bd, ltri = prm
    size = xs[0][0].shape[0]
    d_r = vecs.shape[1]
    w0, a0, k_k, k_a, r_k = [vecs[i:i + 1, :] for i in range(5)]
    flat = [x for chunks in xs for x in chunks]
    x = jnp.concatenate(flat, axis=0) if len(flat) > 1 else flat[0]
    r, k, v = x[:, 0:d_r], x[:, d_r:2 * d_r], x[:, 2 * d_r:3 * d_r]
    lora = x[:, 3 * d_r:3 * d_r + LANES]
    gd = x[:, 3 * d_r + LANES:]
    lw = -float(np.exp(-0.5)) * _sigmoid(w0 + _mm(jnp.tanh(lora), wd_pad, pa=DECAY_TERMS[0]))
    a = _sigmoid(a0 + _mm(lora, wa_pad, pa=RATE_TERMS[0]))
    g = _mm(_sigmoid(gd), wg, pa=GATE_TERMS[0])
    kk = k * k_k
    kk = kk * lax.rsqrt(jnp.maximum(_head_sums(kk * kk, bd, SUM_TERMS[0]), 1e-24))
    k2 = k * (1.0 + (a - 1.0) * k_a)
    bv = kk * a
    bonus = _head_sums(r * k2 * r_k, bd, SUM_TERMS[1]) * v
    out = []
    for ci in range(x.shape[0] // size):
        rows = slice(ci * size, (ci + 1) * size)
        lw_c = lw[rows]
        c = _mm(ltri, lw_c, pb=CUMSUM_TERMS)
        g_last = jnp.exp(c[size - 1:size, :])
        e_neg = jnp.exp(-c)
        e_tail = g_last * e_neg
        full = dict(at=-kk[rows] * jnp.exp(c - lw_c), rt=r[rows] * jnp.exp(c), bc=bv[rows] * e_neg,
                    kc=k2[rows] * e_neg, bh=bv[rows] * e_tail, kh=k2[rows] * e_tail, v=v[rows],
                    g_last=g_last)
        items = [{name: val[:, gi * GROUP:(gi + 1) * GROUP] for name, val in full.items()}
                 for gi in range(d_r // GROUP)]
        out.append((items, bonus[rows], g[rows]))
    return out


def _rwkv_finish(ys, bonus, g, prm):
    vecs, bd = prm[0], prm[4]
    lnx_g, lnx_b = vecs[5:6, :], vecs[6:7, :]
    inv = 1.0 / HEAD_DIM
    mean = _head_sums(ys, bd, SUM_TERMS[2]) * inv
    dlt = ys - mean
    var = _head_sums(dlt * dlt, bd, SUM_TERMS[3]) * inv
    y = dlt * lax.rsqrt(var + LNX_EPS) * lnx_g + lnx_b
    return (y + bonus) * g


def _rwkv_chunks(xs, zs, prm):
    n_seq, n_chunks, n_groups = len(xs), len(xs[0]), len(zs[0])
    size = xs[0][0].shape[0]
    preps = _rwkv_prep(xs, prm)
    local = _rwkv_local([it for items, _, _ in preps for it in items])
    zs = [list(z) for z in zs]
    outs = [[None] * n_chunks for _ in range(n_seq)]
    for ci in range(n_chunks):
        ys = [[None] * n_groups for _ in range(n_seq)]
        for b in range(n_seq):
            for gi in range(n_groups):
                q, y0, g_mat, d_mat = local[(b * n_chunks + ci) * n_groups + gi]
                z_bd = _block_diag(_split(zs[b][gi], STATE_TERMS[1]))
                qz, gz = _mm_rows([q, g_mat], z_bd, pa=STATE_TERMS[0], pb=STATE_TERMS[1])
                ys[b][gi] = qz + y0
                zs[b][gi] = gz + d_mat
        y_all = jnp.concatenate([jnp.concatenate(ys[b], axis=1) for b in range(n_seq)], axis=0)
        bonus = jnp.concatenate([preps[b * n_chunks + ci][1] for b in range(n_seq)], axis=0)
        gate = jnp.concatenate([preps[b * n_chunks + ci][2] for b in range(n_seq)], axis=0)
        o_all = _rwkv_finish(y_all, bonus, gate, prm)
        for b in range(n_seq):
            outs[b][ci] = o_all[b * size:(b + 1) * size]
    return outs, zs


def _rwkv_body(x_ref, vec_ref, wd_ref, wa_ref, wg_ref, bd_ref, lt_ref, o_ref, z_ref):
    @pl.when(pl.program_id(0) == 0)
    def _():
        z_ref[...] = jnp.zeros_like(z_ref)

    terms = lambda ref: [ref[i] for i in range(ref.shape[0])]
    prm = (vec_ref[...], terms(wd_ref), terms(wa_ref), terms(wg_ref), bd_ref[...], lt_ref[...])
    n_seq, n_groups = z_ref.shape[0], z_ref.shape[1]
    n_chunks = x_ref.shape[1] // CHUNK
    xs = [[x_ref[b, ci * CHUNK:(ci + 1) * CHUNK, :] for ci in range(n_chunks)] for b in range(n_seq)]
    zs = [[z_ref[b, gi] for gi in range(n_groups)] for b in range(n_seq)]
    outs, zs = _rwkv_chunks(xs, zs, prm)
    for b in range(n_seq):
        for ci in range(n_chunks):
            o_ref[b, ci * CHUNK:(ci + 1) * CHUNK, :] = outs[b][ci].astype(o_ref.dtype)
        for gi in range(n_groups):
            z_ref[b, gi] = zs[b][gi]


def _bf16_terms(w, n):
    return jnp.stack(_split(w.astype(F32), n))


def _rwkv_call(x, vecs, wd_pad, wa_pad, wg, chunks_per_step=2):
    bsz, seq, n_cols = x.shape
    d_r = vecs.shape[1]
    ts = CHUNK * chunks_per_step
    head = np.arange(GROUP) // HEAD_DIM
    bd = jnp.asarray(head[:, None] == head[None, :], BF16)
    ltri = jnp.asarray(np.tril(np.ones((CHUNK, CHUNK))), BF16)
    full = lambda a: pl.BlockSpec(a.shape, lambda i: (0,) * a.ndim)
    consts = (vecs, _bf16_terms(wd_pad, DECAY_TERMS[1]), _bf16_terms(wa_pad, RATE_TERMS[1]),
              _bf16_terms(wg, GATE_TERMS[1]), bd, ltri)
    return pl.pallas_call(
        _rwkv_body,
        out_shape=jax.ShapeDtypeStruct((bsz, seq, d_r), BF16),
        grid=(seq // ts,),
        in_specs=[pl.BlockSpec((bsz, ts, n_cols), lambda i: (0, i, 0))] + [full(a) for a in consts],
        out_specs=pl.BlockSpec((bsz, ts, d_r), lambda i: (0, i, 0)),
        scratch_shapes=[pltpu.VMEM((bsz, d_r // GROUP, CHUNK, GROUP), F32)],
        compiler_params=pltpu.CompilerParams(dimension_semantics=("arbitrary",),
                                             vmem_limit_bytes=VMEM_LIMIT),
        name="rwkv7_chunked",
    )(x, *consts)


UNIT_CHUNKS = 2
UNIT = UNIT_CHUNKS * CHUNK
WINDOW = BAND + (UNIT_CHUNKS - 1) * CHUNK


def _attn_body(q_ref, kp_ref, kc_ref, vp_ref, vc_ref, bias_ref, o_ref, kn_ref, vn_ref):
    tq = q_ref.shape[1]
    d_a = q_ref.shape[2]
    kn_ref[0:tq, :] = kp_ref[0]
    kn_ref[tq:, :] = kc_ref[0]
    vn_ref[0:tq, :] = vp_ref[0]
    vn_ref[tq:, :] = vc_ref[0]
    lane = _iota((UNIT, LANES), 1)
    head0 = lane < HEAD_DIM
    left = tq - N_LEFT_CHUNKS * CHUNK

    def scores(ui, pr, first_tile):
        sl = slice(pr * LANES, (pr + 1) * LANES)
        qp = q_ref[0, ui * UNIT:(ui + 1) * UNIT, sl]
        kb = kn_ref[ui * UNIT + left:ui * UNIT + left + WINDOW, sl]
        zero = jnp.zeros_like(qp)
        q2 = jnp.concatenate([jnp.where(head0, qp, zero), jnp.where(head0, zero, qp)], axis=0)
        s2 = _mm(q2, kb, _NT)
        n_before = tq - left - ui * UNIT
        es = []
        for h in range(2):
            s = s2[h * UNIT:(h + 1) * UNIT] + bias_ref[2 * pr + h]
            if first_tile and n_before > 0:
                s = jnp.where(_iota(s.shape, 1) < n_before, -1e30, s)
            e = jnp.exp2(s - jnp.max(s, axis=-1, keepdims=True))
            es.append((e.astype(BF16), jnp.sum(e, axis=-1, keepdims=True)))
        return es

    def weighted_values(ui, pr, es):
        sl = slice(pr * LANES, (pr + 1) * LANES)
        vb = vn_ref[ui * UNIT + left:ui * UNIT + left + WINDOW, sl]
        o2 = _mm(jnp.concatenate([e for e, _ in es], axis=0), vb)
        outs = [o2[h * UNIT:(h + 1) * UNIT] / es[h][1] for h in range(2)]
        o_ref[0, ui * UNIT:(ui + 1) * UNIT, sl] = jnp.where(head0, outs[0], outs[1]).astype(o_ref.dtype)

    def tile(first_tile):
        units = [(ui, pr) for ui in range(tq // UNIT) for pr in range(d_a // LANES)]
        pending = None
        for unit in units:
            es = scores(*unit, first_tile)
            if pending is not None:
                weighted_values(*pending)
            pending = (*unit, es)
        weighted_values(*pending)

    pl.when(pl.program_id(1) == 0)(lambda: tile(True))
    pl.when(pl.program_id(1) != 0)(lambda: tile(False))


def _attn_call(qkv, bias, tq=512):
    bsz, seq, three_d = qkv.shape
    d_a = three_d // 3
    assert tq >= N_LEFT_CHUNKS * CHUNK and seq % tq == 0 and tq % UNIT == 0
    cur = lambda col: pl.BlockSpec((1, tq, d_a), lambda b, i: (b, i, col))
    prev = lambda col: pl.BlockSpec((1, tq, d_a), lambda b, i: (b, jnp.maximum(i - 1, 0), col))
    full = lambda a: pl.BlockSpec(a.shape, lambda b, i: (0,) * a.ndim)
    return pl.pallas_call(
        _attn_body,
        out_shape=jax.ShapeDtypeStruct((bsz, seq, d_a), BF16),
        grid=(bsz, seq // tq),
        in_specs=[cur(0), prev(1), cur(1), prev(2), cur(2), full(bias)],
        out_specs=pl.BlockSpec((1, tq, d_a), lambda b, i: (b, i, 0)),
        scratch_shapes=[pltpu.VMEM((2 * tq, d_a), BF16), pltpu.VMEM((2 * tq, d_a), BF16)],
        compiler_params=pltpu.CompilerParams(dimension_semantics=("parallel", "parallel"),
                                             vmem_limit_bytes=VMEM_LIMIT),
        name="chunk_band_attn",
    )(qkv, qkv, qkv, qkv, qkv, bias)


def _rel_bias_table(rel_bias):
    top = N_LEFT_CHUNKS * CHUNK + UNIT - 1
    near = rel_bias[:, 0:2 * REL_CLIP]
    ext = jnp.concatenate([jnp.repeat(rel_bias[:, 2 * REL_CLIP:], top - REL_CLIP + 1, axis=1),
                           near[:, ::-1]], axis=1)
    rows = [ext[:, UNIT - 1 - r:UNIT - 1 - r + WINDOW] for r in range(UNIT)]
    table = jnp.stack(rows, axis=1).astype(F32) * LOG2_E
    q_chunk = np.arange(UNIT)[:, None] // CHUNK
    k_chunk = np.arange(WINDOW)[None, :] // CHUNK
    in_band = (k_chunk >= q_chunk) & (k_chunk <= q_chunk + N_LEFT_CHUNKS)
    return jnp.where(in_band[None], table, -1e30)


def kernel(x, c, w_ada, b_ada, norm1_g, ffn1_w1, ffn1_w3, ffn1_w2, norm2_g, w_in, mu_shift, w0, w_decay_up, a0, w_a_up, w_g_up, k_k, k_a, r_k, lnx_g, lnx_b, q_norm_g, k_norm_g, rel_bias, w_out, norm3_g, ffn2_w1, ffn2_w3, ffn2_w2):
    bsz, seq, d = x.shape
    depth = w_ada.shape[0]
    d_r = w0.shape[1]
    n_lora_w, n_lora_a = w_decay_up.shape[1], w_a_up.shape[1]
    assert n_lora_w + n_lora_a == LANES and w_g_up.shape[1] == LANES
    n_rwkv = 3 * d_r + n_lora_w + n_lora_a + w_g_up.shape[1]
    h = x
    for l in range(depth):
        mod = _mod_call(c, w_ada[l], b_ada[l]).reshape(bsz, N_MOD, d)
        h = _ffn_call(h, mod, norm1_g[l], ffn1_w1[l], ffn1_w3[l], ffn1_w2[l])
        p, qkv = _proj_call(h, mod, norm2_g[l], w_in[l], n_rwkv, q_norm_g[l], k_norm_g[l],
                            mu_shift[l].reshape(1, -1))
        zeros_w = jnp.zeros((n_lora_a, d_r), F32)
        zeros_a = jnp.zeros((n_lora_w, d_r), F32)
        vecs = jnp.stack([w0[l], a0[l], k_k[l], k_a[l], r_k[l].reshape(-1), lnx_g[l], lnx_b[l],
                          jnp.zeros_like(w0[l])]).astype(F32)
        y_r = _rwkv_call(p, vecs,
                         jnp.concatenate([w_decay_up[l], zeros_w], axis=0),
                         jnp.concatenate([zeros_a, w_a_up[l]], axis=0), w_g_up[l])
        y_a = _attn_call(qkv, _rel_bias_table(rel_bias[l]))
        h = _ffn_call(h, mod, norm3_g[l], ffn2_w1[l], ffn2_w3[l], ffn2_w2[l],
                      premix=(y_r, y_a, w_out[l]))
    return h
```
